```python
import jax, jax.numpy as jnp
from jax import lax
import numpy as np

D_MODEL = 2048
BATCH = 2
SEQ = 4096
DEPTH = 1
DEC_BATCH = 8
DEC_SEQ = 16
PAST_LEN = 1024

CHUNK = 64
LEFT_CHUNKS = 8
BAND_LEFT = LEFT_CHUNKS * CHUNK
N_HEADS = 16
HEAD_DIM = 128
D_ATTN = N_HEADS * HEAD_DIM
MAX_REL = 256
N_REL = CHUNK + MAX_REL
D_RNN = 2048
RG_BLOCKS = 16
RG_BLOCK_DIM = D_RNN // RG_BLOCKS
CONV_WIDTH = 4
RG_C = 8.0
D_FF = -(-8 * D_MODEL // (3 * 256)) * 256
D_IN_TOTAL = 3 * D_ATTN + 2 * D_RNN + 2 * D_MODEL
NORM_EPS = 1e-6
NEG_INF = -1e30

kernel_name = "hybrid_chunk_attn_rglru_adaln_stream_step"


def rms_norm(x):
    xf = x.astype(jnp.float32)
    y = xf * lax.rsqrt(jnp.mean(xf * xf, axis=-1, keepdims=True) + NORM_EPS)
    return y.astype(x.dtype)


def adaln(c, w_mod, b_mod):
    m = jax.nn.silu(c) @ w_mod + b_mod
    return jnp.split(m[:, None, :], 6, axis=-1)


def band_attention(q, k, v, q_pos, k_pos, rel_bias):
    diff_c = q_pos[:, :, None] // CHUNK - k_pos[:, None, :] // CHUNK
    valid = (k_pos[:, None, :] >= 0) & (diff_c >= 0) & (diff_c <= LEFT_CHUNKS)
    rel = jnp.clip(q_pos[:, :, None] - k_pos[:, None, :], -(CHUNK - 1), MAX_REL) + (CHUNK - 1)
    bias = jnp.moveaxis(jnp.take(rel_bias, rel, axis=1), 0, 1).astype(jnp.float32)
    s = jnp.einsum('bnqhd,bnkhd->bnhqk', q, k).astype(jnp.float32) * (HEAD_DIM ** -0.5)
    s = jnp.where(valid[None, :, None], s + bias[None], NEG_INF)
    p = jax.nn.softmax(s, axis=-1).astype(v.dtype)
    return jnp.einsum('bnhqk,bnkhd->bnqhd', p, v)


def attn_prompt(q, k, v, rel_bias):
    B, T = q.shape[0], q.shape[1]
    NC = T // CHUNK
    pad = ((0, 0), (BAND_LEFT, 0), (0, 0), (0, 0))
    k_pad = jnp.pad(k, pad)
    v_pad = jnp.pad(v, pad)
    idx = (jnp.arange(NC) * CHUNK)[:, None] + jnp.arange(BAND_LEFT + CHUNK)[None, :]
    k_band = k_pad[:, idx]
    v_band = v_pad[:, idx]
    q_pos = jnp.arange(T).reshape(NC, CHUNK)
    o = band_attention(q.reshape(B, NC, CHUNK, N_HEADS, HEAD_DIM), k_band, v_band,
                       q_pos, idx - BAND_LEFT, rel_bias)
    keep = min(BAND_LEFT, T)
    return o.reshape(B, T, D_ATTN), k[:, T - keep:], v[:, T - keep:]


def attn_sample(q, k, v, cache_k, cache_v, rel_bias):
    B, T = q.shape[0], q.shape[1]
    R = cache_k.shape[1]
    k_all = jnp.concatenate([cache_k, k], axis=1)[:, None]
    v_all = jnp.concatenate([cache_v, v], axis=1)[:, None]
    q_pos = (PAST_LEN + jnp.arange(T))[None, :]
    k_pos = (PAST_LEN - R + jnp.arange(R + T))[None, :]
    o = band_attention(q[:, None], k_all, v_all, q_pos, k_pos, rel_bias)
    return o.reshape(B, T, D_ATTN), k, v


def _scan_combine(left, right):
    a_l, b_l = left
    a_r, b_r = right
    return a_l * a_r, a_r * b_l + b_r


def rglru_branch(u, g, conv_buf, h0, p):
    B, T, _ = u.shape
    u_ext = jnp.concatenate([conv_buf.astype(u.dtype), u], axis=1)
    xc = p['conv_b'] + sum(u_ext[:, j:j + T] * p['conv_w'][j] for j in range(CONV_WIDTH))
    new_buf = u_ext[:, T:]
    xb = xc.reshape(B, T, RG_BLOCKS, RG_BLOCK_DIM)
    r = jax.nn.sigmoid(jnp.einsum('btnd,nde->btne', xb, p['w_rg_a']).reshape(B, T, D_RNN) + p['b_rg_a'])
    i = jax.nn.sigmoid(jnp.einsum('btnd,nde->btne', xb, p['w_rg_x']).reshape(B, T, D_RNN) + p['b_rg_x'])
    log_a = (-RG_C * r.astype(jnp.float32)) * jax.nn.softplus(-p['rg_lambda'].astype(jnp.float32))
    a = jnp.exp(log_a)
    b = jnp.sqrt(-jnp.expm1(2.0 * log_a)) * (i * xc).astype(jnp.float32)
    A, Bc = lax.associative_scan(_scan_combine, (a, b), axis=1)
    h = A * h0.astype(jnp.float32)[:, None] + Bc
    y = h.astype(u.dtype) * jax.nn.gelu(g)
    return y, new_buf, h[:, -1].astype(u.dtype)


def encoder_layer(x, c, p, attn_fn, conv_buf, h0):
    B, T, _ = x.shape
    sh1, sc1, g1, sh2, sc2, g2 = adaln(c, p['w_mod'], p['b_mod'])
    h = rms_norm(x) * (1.0 + sc1) + sh1
    z = h @ p['w_in']
    splits = [D_ATTN, 2 * D_ATTN, 3 * D_ATTN, 3 * D_ATTN + D_RNN,
              3 * D_ATTN + 2 * D_RNN, 3 * D_ATTN + 2 * D_RNN + D_MODEL]
    q, k, v, u, g_rnn, gate_a, gate_r = jnp.split(z, splits, axis=-1)
    hs = (B, T, N_HEADS, HEAD_DIM)
    o_attn, new_k, new_v = attn_fn(q.reshape(hs), k.reshape(hs), v.reshape(hs))
    y_rnn, new_buf, h_last = rglru_branch(u, g_rnn, conv_buf, h0, p)
    merged = (jax.nn.sigmoid(gate_a) * (o_attn @ p['w_proj_attn'])
              + jax.nn.sigmoid(gate_r) * (y_rnn @ p['w_proj_rnn']))
    x = x + g1 * (merged @ p['w_out'])
    h = rms_norm(x) * (1.0 + sc2) + sh2
    ff = (jax.nn.silu(h @ p['w_ffn_gate']) * (h @ p['w_ffn_up'])) @ p['w_ffn_down']
    x = x + g2 * ff
    return x, new_k, new_v, h_last, new_buf


def setup_inputs(seed: int = 0) -> dict:
    key = jax.random.key(seed)
    ks = jax.random.split(key, 32)
    f32 = jnp.float32
    cache_rows = min(BAND_LEFT, PAST_LEN)

    def nrm(k, shape, scale):
        return jax.random.normal(k, shape, f32) * scale

    u_lam = jax.random.uniform(ks[14], (DEPTH, D_RNN), f32, 0.9, 0.999)
    a_base = u_lam ** (1.0 / RG_C)
    rg_lambda = jnp.log(a_base) - jnp.log1p(-a_base)
    return {
        'x_prompt': nrm(ks[0], (BATCH, SEQ, D_MODEL), 1.0),
        'x_sample': nrm(ks[1], (DEC_BATCH, DEC_SEQ, D_MODEL), 1.0),
        'c_prompt': nrm(ks[2], (BATCH, D_MODEL), 1.0),
        'c_sample': nrm(ks[3], (DEC_BATCH, D_MODEL), 1.0),
        'cache_attn_k': nrm(ks[4], (DEPTH, DEC_BATCH, cache_rows, N_HEADS, HEAD_DIM), 1.0),
        'cache_attn_v': nrm(ks[5], (DEPTH, DEC_BATCH, cache_rows, N_HEADS, HEAD_DIM), 1.0),
        'state_rglru_h': nrm(ks[6], (DEPTH, DEC_BATCH, D_RNN), 0.5),
        'state_rglru_conv': nrm(ks[7], (DEPTH, DEC_BATCH, CONV_WIDTH - 1, D_RNN), 1.0),
        'w_mod': nrm(ks[8], (DEPTH, D_MODEL, 6 * D_MODEL), 0.5 * D_MODEL ** -0.5),
        'b_mod': nrm(ks[9], (DEPTH, 6 * D_MODEL), 0.02),
        'w_in': nrm(ks[10], (DEPTH, D_MODEL, D_IN_TOTAL), D_MODEL ** -0.5),
        'rel_bias': nrm(ks[11], (DEPTH, N_HEADS, N_REL), 0.1),
        'conv_w': nrm(ks[12], (DEPTH, CONV_WIDTH, D_RNN), CONV_WIDTH ** -0.5),
        'conv_b': nrm(ks[13], (DEPTH, D_RNN), 0.01),
        'w_rg_a': nrm(ks[15], (DEPTH, RG_BLOCKS, RG_BLOCK_DIM, RG_BLOCK_DIM), RG_BLOCK_DIM ** -0.5),
        'b_rg_a': nrm(ks[16], (DEPTH, D_RNN), 0.01),
        'w_rg_x': nrm(ks[17], (DEPTH, RG_BLOCKS, RG_BLOCK_DIM, RG_BLOCK_DIM), RG_BLOCK_DIM ** -0.5),
        'b_rg_x': nrm(ks[18], (DEPTH, D_RNN), 0.01),
        'rg_lambda': rg_lambda,
        'w_proj_attn': nrm(ks[19], (DEPTH, D_ATTN, D_MODEL), D_ATTN ** -0.5),
        'w_proj_rnn': nrm(ks[20], (DEPTH, D_RNN, D_MODEL), D_RNN ** -0.5),
        'w_out': nrm(ks[21], (DEPTH, D_MODEL, D_MODEL), D_MODEL ** -0.5),
        'w_ffn_gate': nrm(ks[22], (DEPTH, D_MODEL, D_FF), D_MODEL ** -0.5),
        'w_ffn_up': nrm(ks[23], (DEPTH, D_MODEL, D_FF), D_MODEL ** -0.5),
        'w_ffn_down': nrm(ks[24], (DEPTH, D_FF, D_MODEL), D_FF ** -0.5),
        'final_norm_g': 1.0 + nrm(ks[25], (D_MODEL,), 0.05),
    }


def reference(x_prompt, x_sample, c_prompt, c_sample, cache_attn_k, cache_attn_v,
              state_rglru_h, state_rglru_conv, w_mod, b_mod, w_in, rel_bias, conv_w, conv_b,
              w_rg_a, b_rg_a, w_rg_x, b_rg_x, rg_lambda, w_proj_attn, w_proj_rnn, w_out,
              w_ffn_gate, w_ffn_up, w_ffn_down, final_norm_g):
    xp, xs = x_prompt, x_sample
    pk, pv, ph, pc, sk, sv, sh, sc = [], [], [], [], [], [], [], []
    for l in range(DEPTH):
        p = {'w_mod': w_mod[l], 'b_mod': b_mod[l], 'w_in': w_in[l], 'conv_w': conv_w[l],
             'conv_b': conv_b[l], 'w_rg_a': w_rg_a[l], 'b_rg_a': b_rg_a[l], 'w_rg_x': w_rg_x[l],
             'b_rg_x': b_rg_x[l], 'rg_lambda': rg_lambda[l], 'w_proj_attn': w_proj_attn[l],
             'w_proj_rnn': w_proj_rnn[l], 'w_out': w_out[l], 'w_ffn_gate': w_ffn_gate[l],
             'w_ffn_up': w_ffn_up[l], 'w_ffn_down': w_ffn_down[l]}
        rb = rel_bias[l]
        ck, cv = cache_attn_k[l], cache_attn_v[l]
        zero_buf = jnp.zeros((xp.shape[0], CONV_WIDTH - 1, D_RNN), xp.dtype)
        zero_h = jnp.zeros((xp.shape[0], D_RNN), xp.dtype)
        xp, k1, v1, h1, b1 = encoder_layer(
            xp, c_prompt, p, lambda q, k, v: attn_prompt(q, k, v, rb), zero_buf, zero_h)
        xs, k2, v2, h2, b2 = encoder_layer(
            xs, c_sample, p, lambda q, k, v: attn_sample(q, k, v, ck, cv, rb),
            state_rglru_conv[l], state_rglru_h[l])
        pk.append(k1); pv.append(v1); ph.append(h1); pc.append(b1)
        sk.append(k2); sv.append(v2); sh.append(h2); sc.append(b2)
    y_prompt = rms_norm(xp) * final_norm_g
    y_sample = rms_norm(xs) * final_norm_g
    return (y_prompt, y_sample,
            jnp.stack(pk), jnp.stack(pv), jnp.stack(ph), jnp.stack(pc),
            jnp.stack(sk), jnp.stack(sv), jnp.stack(sh), jnp.stack(sc))
```

```python
import functools

import numpy as np
import jax
import jax.numpy as jnp
from jax import lax
from jax.experimental import pallas as pl
from jax.experimental.pallas import tpu as pltpu

F32 = jnp.float32
BF16 = jnp.bfloat16

D_MODEL = 2048
CHUNK = 64
LEFT_CHUNKS = 8
BAND_LEFT = LEFT_CHUNKS * CHUNK
N_HEADS = 16
HEAD_DIM = 128
D_ATTN = N_HEADS * HEAD_DIM
MAX_REL = 256
N_REL = CHUNK + MAX_REL
D_RNN = 2048
RG_BLOCKS = 16
RG_BLOCK_DIM = D_RNN // RG_BLOCKS
CONV_WIDTH = 4
RG_C = 8.0
PAST_LEN = 1024
NORM_EPS = 1e-6
NEG_INF = -1e30

LANES = 128
SUBLANES = 8
VMEM_LIMIT = 56 * 1024 * 1024

OFF_K, OFF_V, OFF_U, OFF_G, OFF_GA, OFF_GR = (i * 2048 for i in range(6))

Q_PAIR = 2 * CHUNK
K_WIN = BAND_LEFT + Q_PAIR
BIAS_W = K_WIN + LANES


def _params(n_axes):
    return pltpu.CompilerParams(dimension_semantics=("arbitrary",) * n_axes,
                                vmem_limit_bytes=VMEM_LIMIT)


def _mod_kernel(c_ref, w_ref, b_ref, o_ref):
    c = c_ref[...]
    s = (c * jax.nn.sigmoid(c)).astype(BF16)
    o_ref[...] = jnp.dot(s, w_ref[...].astype(BF16), preferred_element_type=F32) + b_ref[...]


def _modulation(c_all, w_mod, b_mod, tn=1024):
    rows, d = c_all.shape
    n = w_mod.shape[1]
    return pl.pallas_call(
        _mod_kernel,
        grid=(n // tn,),
        in_specs=[pl.BlockSpec((rows, d), lambda j: (0, 0)),
                  pl.BlockSpec((d, tn), lambda j: (0, j)),
                  pl.BlockSpec((1, tn), lambda j: (0, j))],
        out_specs=pl.BlockSpec((rows, tn), lambda j: (0, j)),
        out_shape=jax.ShapeDtypeStruct((rows, n), F32),
        compiler_params=_params(1),
        name="adaln_mod",
    )(c_all, w_mod, b_mod)


def _rms(xf):
    return xf * lax.rsqrt(jnp.mean(xf * xf, axis=-1, keepdims=True) + NORM_EPS)


def _prenorm_kernel(x_ref, sc_ref, sh_ref, o_ref):
    o_ref[0] = (_rms(x_ref[0]) * (1.0 + sc_ref[0]) + sh_ref[0]).astype(o_ref.dtype)


def _prenorm(x3, sc, sh, tm):
    b, t, d = x3.shape
    return pl.pallas_call(
        _prenorm_kernel,
        grid=(b, t // tm),
        in_specs=[pl.BlockSpec((1, tm, d), lambda i, j: (i, j, 0)),
                  pl.BlockSpec((1, 1, d), lambda i, j: (i, 0, 0)),
                  pl.BlockSpec((1, 1, d), lambda i, j: (i, 0, 0))],
        out_specs=pl.BlockSpec((1, tm, d), lambda i, j: (i, j, 0)),
        out_shape=jax.ShapeDtypeStruct((b, t, d), BF16),
        compiler_params=_params(2),
        name="prenorm",
    )(x3, sc, sh)


def _final_norm_kernel(x_ref, g_ref, o_ref):
    o_ref[...] = _rms(x_ref[...]) * g_ref[...]


def _final_norm(x2, gain, tm):
    m, d = x2.shape
    return pl.pallas_call(
        _final_norm_kernel,
        grid=(m // tm,),
        in_specs=[pl.BlockSpec((tm, d), lambda i: (i, 0)),
                  pl.BlockSpec((1, d), lambda i: (0, 0))],
        out_specs=pl.BlockSpec((tm, d), lambda i: (i, 0)),
        out_shape=jax.ShapeDtypeStruct((m, d), F32),
        compiler_params=_params(1),
        name="final_norm",
    )(x2, gain)


def _mm_kernel(*refs, n_a, a_of_w, n_e, n_v, n_out, epilogue):
    n_w = len(a_of_w)
    a_refs = refs[:n_a]
    w_refs = refs[n_a:n_a + n_w]
    e_refs = refs[n_a + n_w:n_a + n_w + n_e]
    v_refs = refs[n_a + n_w + n_e:n_a + n_w + n_e + n_v]
    o_refs = refs[n_a + n_w + n_e + n_v:n_a + n_w + n_e + n_v + n_out]
    wbf_refs = refs[n_a + n_w + n_e + n_v + n_out:]

    @pl.when(pl.program_id(1) == 0)
    def _():
        for w_ref, wbf_ref in zip(w_refs, wbf_refs):
            wbf_ref[...] = w_ref[...].astype(BF16)

    accs = [jnp.dot(a_refs[a_of_w[i]][...], wbf_refs[i][...], preferred_element_type=F32)
            for i in range(n_w)]
    outs = epilogue(accs, [e[...] for e in e_refs], [v[0] for v in v_refs])
    for o_ref, val in zip(o_refs, outs):
        o_ref[...] = val.astype(o_ref.dtype)


def _matmul(acts, weights, a_of_w, w_col_off, n, tn, tm, epilogue, out_dtypes,
            eins=(), e_col_off=(), vecs=(), rows_per_seq=None, name="mm"):
    m = acts[0].shape[0]
    in_specs = []
    for a in acts:
        in_specs.append(pl.BlockSpec((tm, a.shape[1]), lambda j, i: (i, 0)))
    for w, off in zip(weights, w_col_off):
        in_specs.append(pl.BlockSpec((w.shape[0], tn), lambda j, i, o=off // tn: (0, o + j)))
    for e, off in zip(eins, e_col_off):
        in_specs.append(pl.BlockSpec((tm, tn), lambda j, i, o=off // tn: (i, o + j)))
    for v in vecs:
        in_specs.append(pl.BlockSpec((1, 1, tn), lambda j, i: ((i * tm) // rows_per_seq, 0, j)))
    kern = functools.partial(_mm_kernel, n_a=len(acts), a_of_w=tuple(a_of_w), n_e=len(eins),
                             n_v=len(vecs), n_out=len(out_dtypes), epilogue=epilogue)
    outs = pl.pallas_call(
        kern,
        grid=(n // tn, m // tm),
        in_specs=in_specs,
        out_specs=[pl.BlockSpec((tm, tn), lambda j, i: (i, j)) for _ in out_dtypes],
        out_shape=[jax.ShapeDtypeStruct((m, n), dt) for dt in out_dtypes],
        scratch_shapes=[pltpu.VMEM((w.shape[0], tn), BF16) for w in weights],
        compiler_params=_params(2),
        name=name,
    )(*acts, *weights, *eins, *vecs)
    return outs


def _ep_plain(accs, eins, vecs):
    return [accs[0]]


def _ep_merge(accs, eins, vecs):
    return [jax.nn.sigmoid(eins[0]) * accs[0] + jax.nn.sigmoid(eins[1]) * accs[1]]


def _ep_residual(accs, eins, vecs):
    gate = vecs[0] if vecs else eins[1]
    return [eins[0] + gate * accs[0]]


def _ep_swiglu(accs, eins, vecs):
    g = accs[0]
    return [(g * jax.nn.sigmoid(g)) * accs[1]]


def _bias_source_index():
    m = np.arange(BIAS_W)
    rel = np.where(m < K_WIN, BAND_LEFT - m, MAX_REL)
    return np.clip(rel, -(CHUNK - 1), MAX_REL) + (CHUNK - 1)


def _toeplitz_bias(src_row, n_rows):
    t = jnp.broadcast_to(src_row, (n_rows, BIAS_W))
    ri = lax.broadcasted_iota(jnp.int32, (n_rows, BIAS_W), 0)
    for bit in range(max(1, (n_rows - 1).bit_length())):
        t = jnp.where(((ri >> bit) & 1) == 1, pltpu.roll(t, 1 << bit, axis=1), t)
    return t[:, :K_WIN]


def _softmax_pv(s, vw):
    mx = jnp.max(s, axis=-1, keepdims=True)
    e = jnp.exp(s - mx)
    l = jnp.sum(e, axis=-1, keepdims=True)
    pv = jnp.dot(e.astype(BF16), vw, preferred_element_type=F32)
    return pv / l


def _attn_prompt_kernel(src_ref, mask_ref, q_ref, k_ref, v_ref, o_ref, kpad, vpad, bias_scr, *, t):
    bias_scr[...] = _toeplitz_bias(src_ref[0], Q_PAIR) + mask_ref[...]
    zeros = jnp.zeros((BAND_LEFT, HEAD_DIM), BF16)
    kpad[0:BAND_LEFT, :] = zeros
    vpad[0:BAND_LEFT, :] = zeros
    kpad[BAND_LEFT:, :] = k_ref[...].astype(BF16)
    vpad[BAND_LEFT:, :] = v_ref[...].astype(BF16)
    col = lax.broadcasted_iota(jnp.int32, (Q_PAIR, K_WIN), 1)
    scale = HEAD_DIM ** -0.5

    def body(p, carry, *, clip_start):
        r0 = pl.multiple_of(p * Q_PAIR, Q_PAIR)
        q = q_ref[pl.ds(r0, Q_PAIR), :]
        kw = kpad[pl.ds(r0, K_WIN), :]
        vw = vpad[pl.ds(r0, K_WIN), :]
        s = lax.dot_general(q, kw, (((1,), (1,)), ((), ())), preferred_element_type=F32)
        s = s * scale + bias_scr[...]
        if clip_start:
            s = jnp.where(col >= BAND_LEFT - p * Q_PAIR, s, NEG_INF)
        o_ref[pl.ds(r0, Q_PAIR), :] = _softmax_pv(s, vw).astype(o_ref.dtype)
        return carry

    n_pairs = t // Q_PAIR
    n_clip = min(n_pairs, BAND_LEFT // Q_PAIR)
    lax.fori_loop(0, n_clip, functools.partial(body, clip_start=True), 0)
    lax.fori_loop(n_clip, n_pairs, functools.partial(body, clip_start=False), 0)


def _prompt_band_mask():
    i = np.arange(Q_PAIR)[:, None] // CHUNK
    j = np.arange(K_WIN)[None, :] // CHUNK
    d = i + LEFT_CHUNKS - j
    return np.where((d >= 0) & (d <= LEFT_CHUNKS), 0.0, NEG_INF).astype(np.float32)


def _attn_prompt(src, q, zr, b, t):
    assert t % Q_PAIR == 0
    mask = jnp.asarray(_prompt_band_mask())
    kern = functools.partial(_attn_prompt_kernel, t=t)
    return pl.pallas_call(
        kern,
        grid=(b, N_HEADS),
        in_specs=[pl.BlockSpec((1, 1, BIAS_W), lambda i, h: (h, 0, 0)),
                  pl.BlockSpec((Q_PAIR, K_WIN), lambda i, h: (0, 0)),
                  pl.BlockSpec((t, HEAD_DIM), lambda i, h: (i, h)),
                  pl.BlockSpec((t, HEAD_DIM), lambda i, h: (i, OFF_K // HEAD_DIM + h)),
                  pl.BlockSpec((t, HEAD_DIM), lambda i, h: (i, OFF_V // HEAD_DIM + h))],
        out_specs=pl.BlockSpec((t, HEAD_DIM), lambda i, h: (i, h)),
        out_shape=jax.ShapeDtypeStruct((b * t, D_ATTN), BF16),
        scratch_shapes=[pltpu.VMEM((BAND_LEFT + t, HEAD_DIM), BF16),
                        pltpu.VMEM((BAND_LEFT + t, HEAD_DIM), BF16),
                        pltpu.VMEM((Q_PAIR, K_WIN), F32)],
        compiler_params=_params(2),
        name="attn_prompt",
    )(src, mask, q, zr, zr)


def _attn_sample_kernel(src_ref, mask_ref, q_ref, kn_ref, vn_ref, ck_ref, cv_ref, o_ref, kall, vall,
                        *, t, r):
    kall[0:r, :] = ck_ref[0].astype(BF16)
    vall[0:r, :] = cv_ref[0].astype(BF16)
    kall[r:r + t, :] = kn_ref[...].astype(BF16)
    vall[r:r + t, :] = vn_ref[...].astype(BF16)
    zeros = jnp.zeros((K_WIN - r - t, D_ATTN), BF16)
    kall[r + t:, :] = zeros
    vall[r + t:, :] = zeros
    scale = HEAD_DIM ** -0.5
    for h in range(N_HEADS):
        cs = slice(h * HEAD_DIM, (h + 1) * HEAD_DIM)
        bias = _toeplitz_bias(src_ref[h:h + 1, :], t) + mask_ref[...]
        s = lax.dot_general(q_ref[:, cs], kall[:, cs], (((1,), (1,)), ((), ())),
                            preferred_element_type=F32)
        s = s * scale + bias
        o_ref[:, cs] = _softmax_pv(s, vall[:, cs]).astype(o_ref.dtype)


def _sample_mask(t, r):
    q_pos = PAST_LEN + np.arange(t)[:, None]
    j = np.arange(K_WIN)[None, :]
    k_pos = PAST_LEN - r + j
    d = q_pos // CHUNK - k_pos // CHUNK
    valid = (j < r + t) & (k_pos >= 0) & (d >= 0) & (d <= LEFT_CHUNKS)
    return np.where(valid, 0.0, NEG_INF).astype(np.float32)


def _attn_sample(src, q, zr, cache_k, cache_v, b, t):
    r = cache_k.shape[1]
    assert r == BAND_LEFT and r + t <= K_WIN and t % 16 == 0
    mask = jnp.asarray(_sample_mask(t, r))
    kern = functools.partial(_attn_sample_kernel, t=t, r=r)
    return pl.pallas_call(
        kern,
        grid=(b,),
        in_specs=[pl.BlockSpec((N_HEADS, BIAS_W), lambda i: (0, 0)),
                  pl.BlockSpec((t, K_WIN), lambda i: (0, 0)),
                  pl.BlockSpec((t, D_ATTN), lambda i: (i, 0)),
                  pl.BlockSpec((t, D_ATTN), lambda i: (i, OFF_K // D_ATTN)),
                  pl.BlockSpec((t, D_ATTN), lambda i: (i, OFF_V // D_ATTN)),
                  pl.BlockSpec((1, r, D_ATTN), lambda i: (i, 0, 0)),
                  pl.BlockSpec((1, r, D_ATTN), lambda i: (i, 0, 0))],
        out_specs=pl.BlockSpec((t, D_ATTN), lambda i: (i, 0)),
        out_shape=jax.ShapeDtypeStruct((b * t, D_ATTN), BF16),
        scratch_shapes=[pltpu.VMEM((K_WIN, D_ATTN), BF16),
                        pltpu.VMEM((K_WIN, D_ATTN), BF16)],
        compiler_params=_params(1),
        name="attn_sample",
    )(src, mask, q, zr, zr, cache_k, cache_v)


def _softplus(x):
    return jnp.maximum(x, 0.0) + jnp.log1p(jnp.exp(-jnp.abs(x)))


def _rglru_kernel(u_ref, g_ref, cw_ref, cb_ref, wa_ref, ba_ref, wx_ref, bx_ref, lam_ref, cinit_ref,
                  h0_ref, y_ref, hl_ref, cout_ref, ext, a_scr, b_scr, h_scr, hcarry, *, tt, cb):
    ti = pl.program_id(2)
    pad = SUBLANES

    @pl.when(ti == 0)
    def _():
        ext[0:pad, :] = cinit_ref[0]
        hcarry[...] = h0_ref[0]

    @pl.when(ti > 0)
    def _():
        ext[0:pad, :] = ext[tt:tt + pad, :]

    u = u_ref[...]
    ext[pad:pad + tt, :] = u
    cw = cw_ref[...]
    xc = cb_ref[...] + u * cw[CONV_WIDTH - 1:CONV_WIDTH, :]
    for j in range(CONV_WIDTH - 1):
        back = CONV_WIDTH - 1 - j
        xc = xc + ext[pad - back:pad - back + tt, :] * cw[j:j + 1, :]

    xcb = xc.astype(BF16)
    r_parts, i_parts = [], []
    for n in range(cb // RG_BLOCK_DIM):
        cs = slice(n * RG_BLOCK_DIM, (n + 1) * RG_BLOCK_DIM)
        r_parts.append(jnp.dot(xcb[:, cs], wa_ref[n].astype(BF16), preferred_element_type=F32))
        i_parts.append(jnp.dot(xcb[:, cs], wx_ref[n].astype(BF16), preferred_element_type=F32))
    r = jax.nn.sigmoid(jnp.concatenate(r_parts, axis=1) + ba_ref[...])
    i = jax.nn.sigmoid(jnp.concatenate(i_parts, axis=1) + bx_ref[...])
    log_a = (-RG_C * r) * _softplus(-lam_ref[...])
    a = jnp.exp(log_a)
    th = jnp.tanh(log_a)
    b = jnp.sqrt((-2.0 * th) / (1.0 - th)) * (i * xc)

    groups = tt // SUBLANES
    a3 = a.reshape(groups, SUBLANES, cb)
    b3 = b.reshape(groups, SUBLANES, cb)
    si = lax.broadcasted_iota(jnp.int32, (groups, SUBLANES, cb), 1)
    for k in (1, 2, 4):
        keep = si >= k
        a_prev = pltpu.roll(a3, k, axis=1)
        b_prev = pltpu.roll(b3, k, axis=1)
        b3 = jnp.where(keep, a3 * b_prev + b3, b3)
        a3 = jnp.where(keep, a3 * a_prev, a3)
    a_scr[...] = a3
    b_scr[...] = b3

    def step(gi, h_prev):
        hg = a_scr[gi] * h_prev + b_scr[gi]
        h_scr[gi] = hg
        return hg[SUBLANES - 1:SUBLANES, :]

    h_last = lax.fori_loop(0, groups, step, hcarry[...], unroll=min(groups, 8))
    hcarry[...] = h_last
    h = h_scr[...].reshape(tt, cb)
    y_ref[...] = (h * jax.nn.gelu(g_ref[...])).astype(y_ref.dtype)
    hl_ref[0] = h_last
    cout_ref[0] = ext[tt:tt + pad, :]


def _rglru(zr, b, t, tt, cb, conv_w, conv_b, w_rg_a, b_rg_a, w_rg_x, b_rg_x, lam, cinit, h0):
    assert t % tt == 0 and tt % SUBLANES == 0 and D_RNN % cb == 0 and cb % RG_BLOCK_DIM == 0
    nt = t // tt
    nb = cb // RG_BLOCK_DIM
    vec = lambda: pl.BlockSpec((1, cb), lambda i, j, k: (0, j))
    kern = functools.partial(_rglru_kernel, tt=tt, cb=cb)
    return pl.pallas_call(
        kern,
        grid=(b, D_RNN // cb, nt),
        in_specs=[pl.BlockSpec((tt, cb), lambda i, j, k: (i * nt + k, OFF_U // cb + j)),
                  pl.BlockSpec((tt, cb), lambda i, j, k: (i * nt + k, OFF_G // cb + j)),
                  pl.BlockSpec((CONV_WIDTH, cb), lambda i, j, k: (0, j)),
                  vec(),
                  pl.BlockSpec((nb, RG_BLOCK_DIM, RG_BLOCK_DIM), lambda i, j, k: (j, 0, 0)),
                  vec(),
                  pl.BlockSpec((nb, RG_BLOCK_DIM, RG_BLOCK_DIM), lambda i, j, k: (j, 0, 0)),
                  vec(),
                  vec(),
                  pl.BlockSpec((1, SUBLANES, cb), lambda i, j, k: (i, 0, j)),
                  pl.BlockSpec((1, 1, cb), lambda i, j, k: (i, 0, j))],
        out_specs=[pl.BlockSpec((tt, cb), lambda i, j, k: (i * nt + k, j)),
                   pl.BlockSpec((1, 1, cb), lambda i, j, k: (i, 0, j)),
                   pl.BlockSpec((1, SUBLANES, cb), lambda i, j, k: (i, 0, j))],
        out_shape=[jax.ShapeDtypeStruct((b * t, D_RNN), BF16),
                   jax.ShapeDtypeStruct((b, 1, D_RNN), F32),
                   jax.ShapeDtypeStruct((b, SUBLANES, D_RNN), F32)],
        scratch_shapes=[pltpu.VMEM((tt + SUBLANES, cb), F32),
                        pltpu.VMEM((tt // SUBLANES, SUBLANES, cb), F32),
                        pltpu.VMEM((tt // SUBLANES, SUBLANES, cb), F32),
                        pltpu.VMEM((tt // SUBLANES, SUBLANES, cb), F32),
                        pltpu.VMEM((1, cb), F32)],
        compiler_params=_params(3),
        name="rglru",
    )(zr, zr, conv_w, conv_b, w_rg_a, b_rg_a, w_rg_x, b_rg_x, lam, cinit, h0)


def _layer(x3, mods, p, src, attn_fn, cinit, h0, tm, tt):
    b, t, d = x3.shape
    m = b * t
    sh1, sc1, g1, sh2, sc2, g2 = mods
    x2 = x3.reshape(m, d)

    def gated(res, gate):
        if t % tm == 0:
            return dict(eins=[res], e_col_off=[0], vecs=[gate], rows_per_seq=t)
        gate_rows = jnp.broadcast_to(gate, (b, t, d)).reshape(m, d)
        return dict(eins=[res, gate_rows], e_col_off=[0, 0])

    h = _prenorm(x3, sc1, sh1, min(tm, t)).reshape(m, d)
    (q,) = _matmul([h], [p['w_in']], [0], [0], D_ATTN, 1024, tm, _ep_plain, [BF16], name="in_proj_q")
    (zr,) = _matmul([h], [p['w_in']], [0], [D_ATTN], 6 * 2048, 1024, tm, _ep_plain, [F32],
                    name="in_proj_rest")

    o_attn = attn_fn(src, q, zr, b, t)
    y_rnn, h_last, cout = _rglru(zr, b, t, tt, 512, p['conv_w'], p['conv_b'], p['w_rg_a'], p['b_rg_a'],
                                 p['w_rg_x'], p['b_rg_x'], p['rg_lambda'], cinit, h0)

    (merged,) = _matmul([o_attn, y_rnn], [p['w_proj_attn'], p['w_proj_rnn']], [0, 1], [0, 0],
                        D_MODEL, 512, tm, _ep_merge, [BF16], eins=[zr, zr],
                        e_col_off=[OFF_GA, OFF_GR], name="branch_proj")
    (x1,) = _matmul([merged], [p['w_out']], [0], [0], D_MODEL, 1024, tm, _ep_residual, [F32],
                    name="out_proj", **gated(x2, g1))

    h2 = _prenorm(x1.reshape(b, t, d), sc2, sh2, min(tm, t)).reshape(m, d)
    d_ff = p['w_ffn_gate'].shape[1]
    (act,) = _matmul([h2], [p['w_ffn_gate'], p['w_ffn_up']], [0, 0], [0, 0], d_ff, 512, tm,
                     _ep_swiglu, [BF16], name="ffn_up")
    (xo,) = _matmul([act], [p['w_ffn_down']], [0], [0], D_MODEL, 512, tm, _ep_residual, [F32],
                    name="ffn_down", **gated(x1, g2))
    y = _final_norm(xo, p['final_norm_g'], min(tm, m)).reshape(b, t, d)

    new_k = zr[:, OFF_K:OFF_K + D_ATTN].reshape(b, t, N_HEADS, HEAD_DIM)
    new_v = zr[:, OFF_V:OFF_V + D_ATTN].reshape(b, t, N_HEADS, HEAD_DIM)
    return y, new_k, new_v, h_last.reshape(b, D_RNN), cout[:, SUBLANES - (CONV_WIDTH - 1):, :]


def kernel(x_prompt, x_sample, c_prompt, c_sample, cache_attn_k, cache_attn_v, state_rglru_h,
           state_rglru_conv, w_mod, b_mod, w_in, rel_bias, conv_w, conv_b, w_rg_a, b_rg_a, w_rg_x,
           b_rg_x, rg_lambda, w_proj_attn, w_proj_rnn, w_out, w_ffn_gate, w_ffn_up, w_ffn_down,
           final_norm_g):
    assert w_mod.shape[0] == 1, "single-layer step"
    bp, tp, d = x_prompt.shape
    bs, ts, _ = x_sample.shape
    p = {'w_in': w_in[0], 'conv_w': conv_w[0], 'conv_b': conv_b[0][None], 'w_rg_a': w_rg_a[0],
         'b_rg_a': b_rg_a[0][None], 'w_rg_x': w_rg_x[0], 'b_rg_x': b_rg_x[0][None],
         'rg_lambda': rg_lambda[0][None], 'w_proj_attn': w_proj_attn[0], 'w_proj_rnn': w_proj_rnn[0],
         'w_out': w_out[0], 'w_ffn_gate': w_ffn_gate[0], 'w_ffn_up': w_ffn_up[0],
         'w_ffn_down': w_ffn_down[0], 'final_norm_g': final_norm_g[None]}

    n_seq = bp + bs
    rows = -(-n_seq // 16) * 16
    c_all = jnp.concatenate([c_prompt, c_sample, jnp.zeros((rows - n_seq, d), F32)], axis=0)
    mod = _modulation(c_all, w_mod[0], b_mod[0][None])
    mods_p = [mod[:bp, i * d:(i + 1) * d][:, None, :] for i in range(6)]
    mods_s = [mod[bp:n_seq, i * d:(i + 1) * d][:, None, :] for i in range(6)]

    src = jnp.take(rel_bias[0], jnp.asarray(_bias_source_index()), axis=1)

    keep = min(BAND_LEFT, tp)
    zero_conv = jnp.zeros((bp, SUBLANES, D_RNN), F32)
    zero_h = jnp.zeros((bp, 1, D_RNN), F32)
    yp, k1, v1, h1, b1 = _layer(x_prompt, mods_p, p, src[:, None, :], _attn_prompt, zero_conv, zero_h,
                                tm=512, tt=512)

    ck = cache_attn_k[0].reshape(bs, -1, D_ATTN)
    cv = cache_attn_v[0].reshape(bs, -1, D_ATTN)
    conv_state = jnp.pad(state_rglru_conv[0], ((0, 0), (SUBLANES - (CONV_WIDTH - 1), 0), (0, 0)))
    attn_s = lambda s, q, zr, b, t: _attn_sample(s, q, zr, ck, cv, b, t)
    ys, k2, v2, h2, b2 = _layer(x_sample, mods_s, p, src, attn_s, conv_state,
                                state_rglru_h[0][:, None, :], tm=bs * ts, tt=ts)

    return (yp, ys, k1[None, :, tp - keep:], v1[None, :, tp - keep:], h1[None], b1[None],
            k2[None], v2[None], h2[None], b2[None])
```

```python
import functools

import numpy as np
import jax
import jax.numpy as jnp
from jax import lax
from jax.experimental import pallas as pl
from jax.experimental.pallas import tpu as pltpu

F32 = jnp.float32
BF16 = jnp.bfloat16

D_MODEL = 2048
CHUNK = 64
LEFT_CHUNKS = 8
BAND_LEFT = LEFT_CHUNKS * CHUNK
N_HEADS = 16
HEAD_DIM = 128
D_ATTN = N_HEADS * HEAD_DIM
MAX_REL = 256
N_REL = CHUNK + MAX_REL
D_RNN = 2048
RG_BLOCKS = 16
RG_BLOCK_DIM = D_RNN // RG_BLOCKS
CONV_WIDTH = 4
RG_C = 8.0
PAST_LEN = 1024
NORM_EPS = 1e-6
NEG_INF = -1e30
LOG2_E = 1.4426950408889634

LANES = 128
SUBLANES = 8
VMEM_LIMIT = 56 * 1024 * 1024

OFF_K, OFF_V, OFF_U, OFF_G, OFF_GA, OFF_GR = (i * 2048 for i in range(6))

Q_PAIR = 2 * CHUNK
K_WIN = BAND_LEFT + Q_PAIR
BIAS_W = K_WIN + LANES
Q_BLK = 4 * CHUNK
P_WIN = BAND_LEFT + Q_BLK
P_SRC_W = P_WIN + Q_BLK
T_XPOSE = 512


def _params(n_axes):
    return pltpu.CompilerParams(dimension_semantics=("arbitrary",) * n_axes,
                                vmem_limit_bytes=VMEM_LIMIT)


def _mod_kernel(c_ref, w_ref, b_ref, o_ref):
    c = c_ref[...]
    s = (c * jax.nn.sigmoid(c)).astype(BF16)
    o_ref[...] = jnp.dot(s, w_ref[...].astype(BF16), preferred_element_type=F32) + b_ref[...]


def _modulation(c_all, w_mod, b_mod, tn=1024):
    rows, d = c_all.shape
    n = w_mod.shape[1]
    return pl.pallas_call(
        _mod_kernel,
        grid=(n // tn,),
        in_specs=[pl.BlockSpec((rows, d), lambda j: (0, 0)),
                  pl.BlockSpec((d, tn), lambda j: (0, j)),
                  pl.BlockSpec((1, tn), lambda j: (0, j))],
        out_specs=pl.BlockSpec((rows, tn), lambda j: (0, j)),
        out_shape=jax.ShapeDtypeStruct((rows, n), F32),
        compiler_params=_params(1),
        name="adaln_mod",
    )(c_all, w_mod, b_mod)


def _rms(xf):
    return xf * lax.rsqrt(jnp.mean(xf * xf, axis=-1, keepdims=True) + NORM_EPS)


def _prenorm_kernel(x_ref, sc_ref, sh_ref, o_ref):
    o_ref[0] = (_rms(x_ref[0]) * (1.0 + sc_ref[0]) + sh_ref[0]).astype(o_ref.dtype)


def _prenorm(x3, sc, sh, tm):
    b, t, d = x3.shape
    return pl.pallas_call(
        _prenorm_kernel,
        grid=(b, t // tm),
        in_specs=[pl.BlockSpec((1, tm, d), lambda i, j: (i, j, 0)),
                  pl.BlockSpec((1, 1, d), lambda i, j: (i, 0, 0)),
                  pl.BlockSpec((1, 1, d), lambda i, j: (i, 0, 0))],
        out_specs=pl.BlockSpec((1, tm, d), lambda i, j: (i, j, 0)),
        out_shape=jax.ShapeDtypeStruct((b, t, d), BF16),
        compiler_params=_params(2),
        name="prenorm",
    )(x3, sc, sh)


def _final_norm_kernel(x_ref, g_ref, o_ref):
    o_ref[...] = _rms(x_ref[...]) * g_ref[...]


def _final_norm(x2, gain, tm):
    m, d = x2.shape
    return pl.pallas_call(
        _final_norm_kernel,
        grid=(m // tm,),
        in_specs=[pl.BlockSpec((tm, d), lambda i: (i, 0)),
                  pl.BlockSpec((1, d), lambda i: (0, 0))],
        out_specs=pl.BlockSpec((tm, d), lambda i: (i, 0)),
        out_shape=jax.ShapeDtypeStruct((m, d), F32),
        compiler_params=_params(1),
        name="final_norm",
    )(x2, gain)


def _mm_kernel(*refs, n_a, a_of_w, n_e, n_v, n_out, epilogue):
    n_w = len(a_of_w)
    a_refs = refs[:n_a]
    w_refs = refs[n_a:n_a + n_w]
    e_refs = refs[n_a + n_w:n_a + n_w + n_e]
    v_refs = refs[n_a + n_w + n_e:n_a + n_w + n_e + n_v]
    o_refs = refs[n_a + n_w + n_e + n_v:n_a + n_w + n_e + n_v + n_out]
    wbf_refs = refs[n_a + n_w + n_e + n_v + n_out:]

    @pl.when(pl.program_id(1) == 0)
    def _():
        for w_ref, wbf_ref in zip(w_refs, wbf_refs):
            wbf_ref[...] = w_ref[...].astype(BF16)

    accs = [jnp.dot(a_refs[a_of_w[i]][...], wbf_refs[i][...], preferred_element_type=F32)
            for i in range(n_w)]
    outs = epilogue(accs, [e[...] for e in e_refs], [v[0] for v in v_refs])
    for o_ref, val in zip(o_refs, outs):
        o_ref[...] = val.astype(o_ref.dtype)


def _matmul(acts, weights, a_of_w, w_col_off, n, tn, tm, epilogue, out_dtypes,
            eins=(), e_col_off=(), vecs=(), rows_per_seq=None, name="mm"):
    m = acts[0].shape[0]
    in_specs = []
    for a in acts:
        in_specs.append(pl.BlockSpec((tm, a.shape[1]), lambda j, i: (i, 0)))
    for w, off in zip(weights, w_col_off):
        in_specs.append(pl.BlockSpec((w.shape[0], tn), lambda j, i, o=off // tn: (0, o + j)))
    for e, off in zip(eins, e_col_off):
        in_specs.append(pl.BlockSpec((tm, tn), lambda j, i, o=off // tn: (i, o + j)))
    for v in vecs:
        in_specs.append(pl.BlockSpec((1, 1, tn), lambda j, i: ((i * tm) // rows_per_seq, 0, j)))
    kern = functools.partial(_mm_kernel, n_a=len(acts), a_of_w=tuple(a_of_w), n_e=len(eins),
                             n_v=len(vecs), n_out=len(out_dtypes), epilogue=epilogue)
    outs = pl.pallas_call(
        kern,
        grid=(n // tn, m // tm),
        in_specs=in_specs,
        out_specs=[pl.BlockSpec((tm, tn), lambda j, i: (i, j)) for _ in out_dtypes],
        out_shape=[jax.ShapeDtypeStruct((m, n), dt) for dt in out_dtypes],
        scratch_shapes=[pltpu.VMEM((w.shape[0], tn), BF16) for w in weights],
        compiler_params=_params(2),
        name=name,
    )(*acts, *weights, *eins, *vecs)
    return outs


def _ep_plain(accs, eins, vecs):
    return [accs[0]]


def _ep_merge(accs, eins, vecs):
    return [jax.nn.sigmoid(eins[0]) * accs[0] + jax.nn.sigmoid(eins[1]) * accs[1]]


def _ep_residual(accs, eins, vecs):
    gate = vecs[0] if vecs else eins[1]
    return [eins[0] + gate * accs[0]]


def _ep_swiglu(accs, eins, vecs):
    g = accs[0]
    return [(g * jax.nn.sigmoid(g)) * accs[1]]


def _bias_source_index():
    m = np.arange(BIAS_W)
    rel = np.where(m < K_WIN, BAND_LEFT - m, MAX_REL)
    return np.clip(rel, -(CHUNK - 1), MAX_REL) + (CHUNK - 1)


def _toeplitz_bias(src_row, n_rows):
    t = jnp.broadcast_to(src_row, (n_rows, BIAS_W))
    ri = lax.broadcasted_iota(jnp.int32, (n_rows, BIAS_W), 0)
    for bit in range(max(1, (n_rows - 1).bit_length())):
        t = jnp.where(((ri >> bit) & 1) == 1, pltpu.roll(t, 1 << bit, axis=1), t)
    return t[:, :K_WIN]


def _softmax_pv(s, vw):
    mx = jnp.max(s, axis=-1, keepdims=True)
    e = jnp.exp(s - mx)
    l = jnp.sum(e, axis=-1, keepdims=True)
    pv = jnp.dot(e.astype(BF16), vw, preferred_element_type=F32)
    return pv / l


def _prompt_bias_source_index():
    m = np.arange(P_SRC_W)
    rel = np.where(m < Q_BLK, MAX_REL, m - P_SRC_W + BAND_LEFT)
    return np.clip(rel, -(CHUNK - 1), MAX_REL) + (CHUNK - 1)


def _attn_prompt_kernel(src_ref, mask_ref, q_ref, k_ref, v_ref, o_ref, kpad, vtpad, bias_scr, st_scr,
                        *, t):
    @pl.when(pl.program_id(1) == 0)
    def _():
        r8 = jnp.broadcast_to(src_ref[0], (SUBLANES, P_SRC_W))
        ri = lax.broadcasted_iota(jnp.int32, (SUBLANES, P_SRC_W), 0)
        for bit in range(3):
            r8 = jnp.where(((ri >> bit) & 1) == 1, pltpu.roll(r8, 1 << bit, axis=1), r8)
        ext = jnp.concatenate([r8, r8[:, :Q_BLK]], axis=1)
        ext_w = P_SRC_W + Q_BLK
        for t_off in range(0, LANES, SUBLANES):
            rot = ext if t_off == 0 else pltpu.roll(ext, ext_w - t_off, axis=1)
            for a in range(P_WIN // SUBLANES):
                start = (P_SRC_W - SUBLANES * a) % P_SRC_W
                if start % LANES != t_off:
                    continue
                rows = slice(SUBLANES * a, SUBLANES * (a + 1))
                col = start - t_off
                bias_scr[rows, :] = (rot[:, col:col + Q_BLK] + mask_ref[rows, :]) * LOG2_E

    kpad[0:BAND_LEFT, :] = jnp.zeros((BAND_LEFT, HEAD_DIM), BF16)
    vtpad[:, 0:BAND_LEFT] = jnp.zeros((HEAD_DIM, BAND_LEFT), BF16)
    kpad[BAND_LEFT:, :] = k_ref[...].astype(BF16)
    for c in range(t // T_XPOSE):
        rows = slice(c * T_XPOSE, (c + 1) * T_XPOSE)
        vtpad[:, BAND_LEFT + c * T_XPOSE:BAND_LEFT + (c + 1) * T_XPOSE] = v_ref[rows, :].T.astype(BF16)
    row = lax.broadcasted_iota(jnp.int32, (P_WIN, Q_BLK), 0)
    scale = HEAD_DIM ** -0.5
    n_blk = t // Q_BLK
    half = P_WIN // 2

    def scores(blk, slot):
        r0 = pl.multiple_of(blk * Q_BLK, Q_BLK)
        q = q_ref[pl.ds(r0, Q_BLK), :]
        for hh in range(2):
            kw = kpad[pl.ds(pl.multiple_of(r0 + hh * half, LANES), half), :]
            st_scr[slot, hh * half:(hh + 1) * half, :] = lax.dot_general(
                kw, q, (((1,), (1,)), ((), ())), preferred_element_type=F32)

    def finish(blk, slot, clip_start):
        r0 = pl.multiple_of(blk * Q_BLK, Q_BLK)
        st = st_scr[slot] * (scale * LOG2_E) + bias_scr[...]
        if clip_start:
            st = jnp.where(row >= BAND_LEFT - blk * Q_BLK, st, NEG_INF)
        mx = jnp.max(st, axis=0, keepdims=True)
        e = jnp.exp2(st - mx)
        l = jnp.sum(e, axis=0, keepdims=True)
        vtw = vtpad[:, pl.ds(r0, P_WIN)]
        ot = jnp.dot(vtw, e.astype(BF16), preferred_element_type=F32) / l
        o_ref[pl.ds(r0, Q_BLK), :] = ot.T.astype(o_ref.dtype)

    def pair(j, carry, *, clip_start):
        scores(2 * j + 1, 1)
        finish(2 * j, 0, clip_start)
        scores(jnp.minimum(2 * j + 2, n_blk - 1), 0)
        finish(2 * j + 1, 1, clip_start)
        return carry

    assert n_blk % 2 == 0 and BAND_LEFT == 2 * Q_BLK
    scores(0, 0)
    pair(0, 0, clip_start=True)
    lax.fori_loop(1, n_blk // 2, functools.partial(pair, clip_start=False), 0, unroll=True)


def _prompt_band_mask():
    j = np.arange(P_WIN)[:, None] // CHUNK
    i = np.arange(Q_BLK)[None, :] // CHUNK
    d = i + LEFT_CHUNKS - j
    return np.where((d >= 0) & (d <= LEFT_CHUNKS), 0.0, NEG_INF).astype(np.float32)


def _attn_prompt(src, q, zr, b, t):
    assert t % Q_BLK == 0 and t % T_XPOSE == 0
    mask = jnp.asarray(_prompt_band_mask())
    kern = functools.partial(_attn_prompt_kernel, t=t)
    return pl.pallas_call(
        kern,
        grid=(N_HEADS, b),
        in_specs=[pl.BlockSpec((1, 1, P_SRC_W), lambda h, i: (h, 0, 0)),
                  pl.BlockSpec((P_WIN, Q_BLK), lambda h, i: (0, 0)),
                  pl.BlockSpec((t, HEAD_DIM), lambda h, i: (i, h)),
                  pl.BlockSpec((t, HEAD_DIM), lambda h, i: (i, OFF_K // HEAD_DIM + h)),
                  pl.BlockSpec((t, HEAD_DIM), lambda h, i: (i, OFF_V // HEAD_DIM + h))],
        out_specs=pl.BlockSpec((t, HEAD_DIM), lambda h, i: (i, h)),
        out_shape=jax.ShapeDtypeStruct((b * t, D_ATTN), BF16),
        scratch_shapes=[pltpu.VMEM((BAND_LEFT + t, HEAD_DIM), BF16),
                        pltpu.VMEM((HEAD_DIM, BAND_LEFT + t), BF16),
                        pltpu.VMEM((P_WIN, Q_BLK), F32),
                        pltpu.VMEM((2, P_WIN, Q_BLK), F32)],
        compiler_params=_params(2),
        name="attn_prompt",
    )(src, mask, q, zr, zr)


def _attn_sample_kernel(src_ref, mask_ref, q_ref, kn_ref, vn_ref, ck_ref, cv_ref, o_ref, kall, vall,
                        *, t, r):
    kall[0:r, :] = ck_ref[0].astype(BF16)
    vall[0:r, :] = cv_ref[0].astype(BF16)
    kall[r:r + t, :] = kn_ref[...].astype(BF16)
    vall[r:r + t, :] = vn_ref[...].astype(BF16)
    zeros = jnp.zeros((K_WIN - r - t, D_ATTN), BF16)
    kall[r + t:, :] = zeros
    vall[r + t:, :] = zeros
    scale = HEAD_DIM ** -0.5
    for h in range(N_HEADS):
        cs = slice(h * HEAD_DIM, (h + 1) * HEAD_DIM)
        bias = _toeplitz_bias(src_ref[h:h + 1, :], t) + mask_ref[...]
        s = lax.dot_general(q_ref[:, cs], kall[:, cs], (((1,), (1,)), ((), ())),
                            preferred_element_type=F32)
        s = s * scale + bias
        o_ref[:, cs] = _softmax_pv(s, vall[:, cs]).astype(o_ref.dtype)


def _sample_mask(t, r):
    q_pos = PAST_LEN + np.arange(t)[:, None]
    j = np.arange(K_WIN)[None, :]
    k_pos = PAST_LEN - r + j
    d = q_pos // CHUNK - k_pos // CHUNK
    valid = (j < r + t) & (k_pos >= 0) & (d >= 0) & (d <= LEFT_CHUNKS)
    return np.where(valid, 0.0, NEG_INF).astype(np.float32)


def _attn_sample(src, q, zr, cache_k, cache_v, b, t):
    r = cache_k.shape[1]
    assert r == BAND_LEFT and r + t <= K_WIN and t % 16 == 0
    mask = jnp.asarray(_sample_mask(t, r))
    kern = functools.partial(_attn_sample_kernel, t=t, r=r)
    return pl.pallas_call(
        kern,
        grid=(b,),
        in_specs=[pl.BlockSpec((N_HEADS, BIAS_W), lambda i: (0, 0)),
                  pl.BlockSpec((t, K_WIN), lambda i: (0, 0)),
                  pl.BlockSpec((t, D_ATTN), lambda i: (i, 0)),
                  pl.BlockSpec((t, D_ATTN), lambda i: (i, OFF_K // D_ATTN)),
                  pl.BlockSpec((t, D_ATTN), lambda i: (i, OFF_V // D_ATTN)),
                  pl.BlockSpec((1, r, D_ATTN), lambda i: (i, 0, 0)),
                  pl.BlockSpec((1, r, D_ATTN), lambda i: (i, 0, 0))],
        out_specs=pl.BlockSpec((t, D_ATTN), lambda i: (i, 0)),
        out_shape=jax.ShapeDtypeStruct((b * t, D_ATTN), BF16),
        scratch_shapes=[pltpu.VMEM((K_WIN, D_ATTN), BF16),
                        pltpu.VMEM((K_WIN, D_ATTN), BF16)],
        compiler_params=_params(1),
        name="attn_sample",
    )(src, mask, q, zr, zr, cache_k, cache_v)


def _softplus(x):
    return jnp.maximum(x, 0.0) + jnp.log1p(jnp.exp(-jnp.abs(x)))


def _rglru_kernel(u_ref, g_ref, cw_ref, cb_ref, wa_ref, ba_ref, wx_ref, bx_ref, lam_ref, cinit_ref,
                  h0_ref, y_ref, hl_ref, cout_ref, ext, a_scr, b_scr, h_scr, hcarry, *, tt, cb):
    ti = pl.program_id(2)
    pad = SUBLANES

    @pl.when(ti == 0)
    def _():
        ext[0:pad, :] = cinit_ref[0]
        hcarry[...] = h0_ref[0]

    @pl.when(ti > 0)
    def _():
        ext[0:pad, :] = ext[tt:tt + pad, :]

    u = u_ref[...]
    ext[pad:pad + tt, :] = u
    cw = cw_ref[...]
    xc = cb_ref[...] + u * cw[CONV_WIDTH - 1:CONV_WIDTH, :]
    for j in range(CONV_WIDTH - 1):
        back = CONV_WIDTH - 1 - j
        xc = xc + ext[pad - back:pad - back + tt, :] * cw[j:j + 1, :]

    xcb = xc.astype(BF16)
    r_parts, i_parts = [], []
    for n in range(cb // RG_BLOCK_DIM):
        cs = slice(n * RG_BLOCK_DIM, (n + 1) * RG_BLOCK_DIM)
        r_parts.append(jnp.dot(xcb[:, cs], wa_ref[n].astype(BF16), preferred_element_type=F32))
        i_parts.append(jnp.dot(xcb[:, cs], wx_ref[n].astype(BF16), preferred_element_type=F32))
    r = jax.nn.sigmoid(jnp.concatenate(r_parts, axis=1) + ba_ref[...])
    i = jax.nn.sigmoid(jnp.concatenate(i_parts, axis=1) + bx_ref[...])
    log_a = (-RG_C * r) * _softplus(-lam_ref[...])
    a = jnp.exp(log_a)
    th = jnp.tanh(log_a)
    b = jnp.sqrt((-2.0 * th) / (1.0 - th)) * (i * xc)

    groups = tt // SUBLANES
    a3 = a.reshape(groups, SUBLANES, cb)
    b3 = b.reshape(groups, SUBLANES, cb)
    si = lax.broadcasted_iota(jnp.int32, (groups, SUBLANES, cb), 1)
    for k in (1, 2, 4):
        keep = si >= k
        a_prev = pltpu.roll(a3, k, axis=1)
        b_prev = pltpu.roll(b3, k, axis=1)
        b3 = jnp.where(keep, a3 * b_prev + b3, b3)
        a3 = jnp.where(keep, a3 * a_prev, a3)
    a_scr[...] = a3
    b_scr[...] = b3

    def step(gi, h_prev):
        hg = a_scr[gi] * h_prev + b_scr[gi]
        h_scr[gi] = hg
        return hg[SUBLANES - 1:SUBLANES, :]

    h_last = lax.fori_loop(0, groups, step, hcarry[...], unroll=min(groups, 8))
    hcarry[...] = h_last
    h = h_scr[...].reshape(tt, cb)
    y_ref[...] = (h * jax.nn.gelu(g_ref[...])).astype(y_ref.dtype)
    hl_ref[0] = h_last
    cout_ref[0] = ext[tt:tt + pad, :]


def _rglru(zr, b, t, tt, cb, conv_w, conv_b, w_rg_a, b_rg_a, w_rg_x, b_rg_x, lam, cinit, h0):
    assert t % tt == 0 and tt % SUBLANES == 0 and D_RNN % cb == 0 and cb % RG_BLOCK_DIM == 0
    nt = t // tt
    nb = cb // RG_BLOCK_DIM
    vec = lambda: pl.BlockSpec((1, cb), lambda i, j, k: (0, j))
    kern = functools.partial(_rglru_kernel, tt=tt, cb=cb)
    return pl.pallas_call(
        kern,
        grid=(b, D_RNN // cb, nt),
        in_specs=[pl.BlockSpec((tt, cb), lambda i, j, k: (i * nt + k, OFF_U // cb + j)),
                  pl.BlockSpec((tt, cb), lambda i, j, k: (i * nt + k, OFF_G // cb + j)),
                  pl.BlockSpec((CONV_WIDTH, cb), lambda i, j, k: (0, j)),
                  vec(),
                  pl.BlockSpec((nb, RG_BLOCK_DIM, RG_BLOCK_DIM), lambda i, j, k: (j, 0, 0)),
                  vec(),
                  pl.BlockSpec((nb, RG_BLOCK_DIM, RG_BLOCK_DIM), lambda i, j, k: (j, 0, 0)),
                  vec(),
                  vec(),
                  pl.BlockSpec((1, SUBLANES, cb), lambda i, j, k: (i, 0, j)),
                  pl.BlockSpec((1, 1, cb), lambda i, j, k: (i, 0, j))],
        out_specs=[pl.BlockSpec((tt, cb), lambda i, j, k: (i * nt + k, j)),
                   pl.BlockSpec((1, 1, cb), lambda i, j, k: (i, 0, j)),
                   pl.BlockSpec((1, SUBLANES, cb), lambda i, j, k: (i, 0, j))],
        out_shape=[jax.ShapeDtypeStruct((b * t, D_RNN), BF16),
                   jax.ShapeDtypeStruct((b, 1, D_RNN), F32),
                   jax.ShapeDtypeStruct((b, SUBLANES, D_RNN), F32)],
        scratch_shapes=[pltpu.VMEM((tt + SUBLANES, cb), F32),
                        pltpu.VMEM((tt // SUBLANES, SUBLANES, cb), F32),
                        pltpu.VMEM((tt // SUBLANES, SUBLANES, cb), F32),
                        pltpu.VMEM((tt // SUBLANES, SUBLANES, cb), F32),
                        pltpu.VMEM((1, cb), F32)],
        compiler_params=_params(3),
        name="rglru",
    )(zr, zr, conv_w, conv_b, w_rg_a, b_rg_a, w_rg_x, b_rg_x, lam, cinit, h0)


def _layer(x3, mods, p, src, attn_fn, cinit, h0, tm, tt):
    b, t, d = x3.shape
    m = b * t
    sh1, sc1, g1, sh2, sc2, g2 = mods
    x2 = x3.reshape(m, d)

    def gated(res, gate):
        if t % tm == 0:
            return dict(eins=[res], e_col_off=[0], vecs=[gate], rows_per_seq=t)
        gate_rows = jnp.broadcast_to(gate, (b, t, d)).reshape(m, d)
        return dict(eins=[res, gate_rows], e_col_off=[0, 0])

    h = _prenorm(x3, sc1, sh1, min(tm, t)).reshape(m, d)
    (q,) = _matmul([h], [p['w_in']], [0], [0], D_ATTN, 1024, tm, _ep_plain, [BF16], name="in_proj_q")
    (zr,) = _matmul([h], [p['w_in']], [0], [D_ATTN], 6 * 2048, 1024, tm, _ep_plain, [F32],
                    name="in_proj_rest")

    o_attn = attn_fn(src, q, zr, b, t)
    y_rnn, h_last, cout = _rglru(zr, b, t, tt, 512, p['conv_w'], p['conv_b'], p['w_rg_a'], p['b_rg_a'],
                                 p['w_rg_x'], p['b_rg_x'], p['rg_lambda'], cinit, h0)

    (merged,) = _matmul([o_attn, y_rnn], [p['w_proj_attn'], p['w_proj_rnn']], [0, 1], [0, 0],
                        D_MODEL, 512, tm, _ep_merge, [BF16], eins=[zr, zr],
                        e_col_off=[OFF_GA, OFF_GR], name="branch_proj")
    (x1,) = _matmul([merged], [p['w_out']], [0], [0], D_MODEL, 1024, tm, _ep_residual, [F32],
                    name="out_proj", **gated(x2, g1))

    h2 = _prenorm(x1.reshape(b, t, d), sc2, sh2, min(tm, t)).reshape(m, d)
    d_ff = p['w_ffn_gate'].shape[1]
    (act,) = _matmul([h2], [p['w_ffn_gate'], p['w_ffn_up']], [0, 0], [0, 0], d_ff, 512, tm,
                     _ep_swiglu, [BF16], name="ffn_up")
    (xo,) = _matmul([act], [p['w_ffn_down']], [0], [0], D_MODEL, 512, tm, _ep_residual, [F32],
                    name="ffn_down", **gated(x1, g2))
    y = _final_norm(xo, p['final_norm_g'], min(tm, m)).reshape(b, t, d)

    keep = min(BAND_LEFT, t)
    zr3 = zr.reshape(b, t, zr.shape[1])
    new_k = zr3[:, t - keep:, OFF_K:OFF_K + D_ATTN].reshape(b, keep, N_HEADS, HEAD_DIM)
    new_v = zr3[:, t - keep:, OFF_V:OFF_V + D_ATTN].reshape(b, keep, N_HEADS, HEAD_DIM)
    return y, new_k, new_v, h_last.reshape(b, D_RNN), cout[:, SUBLANES - (CONV_WIDTH - 1):, :]


def kernel(x_prompt, x_sample, c_prompt, c_sample, cache_attn_k, cache_attn_v, state_rglru_h,
           state_rglru_conv, w_mod, b_mod, w_in, rel_bias, conv_w, conv_b, w_rg_a, b_rg_a, w_rg_x,
           b_rg_x, rg_lambda, w_proj_attn, w_proj_rnn, w_out, w_ffn_gate, w_ffn_up, w_ffn_down,
           final_norm_g):
    assert w_mod.shape[0] == 1, "single-layer step"
    bp, tp, d = x_prompt.shape
    bs, ts, _ = x_sample.shape
    p = {'w_in': w_in[0], 'conv_w': conv_w[0], 'conv_b': conv_b[0][None], 'w_rg_a': w_rg_a[0],
         'b_rg_a': b_rg_a[0][None], 'w_rg_x': w_rg_x[0], 'b_rg_x': b_rg_x[0][None],
         'rg_lambda': rg_lambda[0][None], 'w_proj_attn': w_proj_attn[0], 'w_proj_rnn': w_proj_rnn[0],
         'w_out': w_out[0], 'w_ffn_gate': w_ffn_gate[0], 'w_ffn_up': w_ffn_up[0],
         'w_ffn_down': w_ffn_down[0], 'final_norm_g': final_norm_g[None]}

    n_seq = bp + bs
    rows = -(-n_seq // 16) * 16
    c_all = jnp.concatenate([c_prompt, c_sample, jnp.zeros((rows - n_seq, d), F32)], axis=0)
    mod = _modulation(c_all, w_mod[0], b_mod[0][None])
    mods_p = [mod[:bp, i * d:(i + 1) * d][:, None, :] for i in range(6)]
    mods_s = [mod[bp:n_seq, i * d:(i + 1) * d][:, None, :] for i in range(6)]

    src = jnp.take(rel_bias[0], jnp.asarray(_bias_source_index()), axis=1)
    src_p = jnp.take(rel_bias[0], jnp.asarray(_prompt_bias_source_index()), axis=1)

    zero_conv = jnp.zeros((bp, SUBLANES, D_RNN), F32)
    zero_h = jnp.zeros((bp, 1, D_RNN), F32)
    yp, k1, v1, h1, b1 = _layer(x_prompt, mods_p, p, src_p[:, None, :], _attn_prompt, zero_conv, zero_h,
                                tm=512, tt=512)

    ck = cache_attn_k[0].reshape(bs, -1, D_ATTN)
    cv = cache_attn_v[0].reshape(bs, -1, D_ATTN)
    conv_state = jnp.pad(state_rglru_conv[0], ((0, 0), (SUBLANES - (CONV_WIDTH - 1), 0), (0, 0)))
    attn_s = lambda s, q, zr, b, t: _attn_sample(s, q, zr, ck, cv, b, t)
    ys, k2, v2, h2, b2 = _layer(x_sample, mods_s, p, src, attn_s, conv_state,
                                state_rglru_h[0][:, None, :], tm=bs * ts, tt=ts)

    return (yp, ys, k1[None], v1[None], h1[None], b1[None], k2[None], v2[None], h2[None], b2[None])
```

```python
import functools

import numpy as np
import jax
import jax.numpy as jnp
from jax import lax
from jax.experimental import pallas as pl
from jax.experimental.pallas import tpu as pltpu

F32 = jnp.float32
BF16 = jnp.bfloat16

D_MODEL = 2048
CHUNK = 64
LEFT_CHUNKS = 8
BAND_LEFT = LEFT_CHUNKS * CHUNK
N_HEADS = 16
HEAD_DIM = 128
D_ATTN = N_HEADS * HEAD_DIM
MAX_REL = 256
N_REL = CHUNK + MAX_REL
D_RNN = 2048
RG_BLOCKS = 16
RG_BLOCK_DIM = D_RNN // RG_BLOCKS
CONV_WIDTH = 4
RG_C = 8.0
PAST_LEN = 1024
NORM_EPS = 1e-6
NEG_INF = -1e30
LOG2_E = 1.4426950408889634

LANES = 128
SUBLANES = 8
VMEM_LIMIT = 56 * 1024 * 1024

OFF_K, OFF_V, OFF_U, OFF_G, OFF_GA, OFF_GR = (i * 2048 for i in range(6))

Q_PAIR = 2 * CHUNK
K_WIN = BAND_LEFT + Q_PAIR
BIAS_W = K_WIN + LANES
Q_BLK = 4 * CHUNK
P_WIN = BAND_LEFT + Q_BLK
P_SRC_W = P_WIN + Q_BLK
T_XPOSE = 512


def _params(n_axes):
    return pltpu.CompilerParams(dimension_semantics=("arbitrary",) * n_axes,
                                vmem_limit_bytes=VMEM_LIMIT)


def _mod_kernel(c_ref, w_ref, b_ref, o_ref):
    c = c_ref[...]
    s = (c * jax.nn.sigmoid(c)).astype(BF16)
    o_ref[...] = jnp.dot(s, w_ref[...].astype(BF16), preferred_element_type=F32) + b_ref[...]


def _modulation(c_all, w_mod, b_mod, tn=1024):
    rows, d = c_all.shape
    n = w_mod.shape[1]
    return pl.pallas_call(
        _mod_kernel,
        grid=(n // tn,),
        in_specs=[pl.BlockSpec((rows, d), lambda j: (0, 0)),
                  pl.BlockSpec((d, tn), lambda j: (0, j)),
                  pl.BlockSpec((1, tn), lambda j: (0, j))],
        out_specs=pl.BlockSpec((rows, tn), lambda j: (0, j)),
        out_shape=jax.ShapeDtypeStruct((rows, n), F32),
        compiler_params=_params(1),
        name="adaln_mod",
    )(c_all, w_mod, b_mod)


def _rms(xf):
    return xf * lax.rsqrt(jnp.mean(xf * xf, axis=-1, keepdims=True) + NORM_EPS)


def _prenorm_kernel(x_ref, sc_ref, sh_ref, o_ref):
    o_ref[0] = (_rms(x_ref[0]) * (1.0 + sc_ref[0]) + sh_ref[0]).astype(o_ref.dtype)


def _prenorm(x3, sc, sh, tm):
    b, t, d = x3.shape
    return pl.pallas_call(
        _prenorm_kernel,
        grid=(b, t // tm),
        in_specs=[pl.BlockSpec((1, tm, d), lambda i, j: (i, j, 0)),
                  pl.BlockSpec((1, 1, d), lambda i, j: (i, 0, 0)),
                  pl.BlockSpec((1, 1, d), lambda i, j: (i, 0, 0))],
        out_specs=pl.BlockSpec((1, tm, d), lambda i, j: (i, j, 0)),
        out_shape=jax.ShapeDtypeStruct((b, t, d), BF16),
        compiler_params=_params(2),
        name="prenorm",
    )(x3, sc, sh)


def _final_norm_kernel(x_ref, g_ref, o_ref):
    o_ref[...] = _rms(x_ref[...]) * g_ref[...]


def _final_norm(x2, gain, tm):
    m, d = x2.shape
    return pl.pallas_call(
        _final_norm_kernel,
        grid=(m // tm,),
        in_specs=[pl.BlockSpec((tm, d), lambda i: (i, 0)),
                  pl.BlockSpec((1, d), lambda i: (0, 0))],
        out_specs=pl.BlockSpec((tm, d), lambda i: (i, 0)),
        out_shape=jax.ShapeDtypeStruct((m, d), F32),
        compiler_params=_params(1),
        name="final_norm",
    )(x2, gain)


def _mm_kernel(*refs, n_a, a_of_w, n_e, n_ex, n_v, n_out, epilogue):
    it = iter(refs)
    take = lambda k: [next(it) for _ in range(k)]
    n_w = len(a_of_w)
    a_refs, ax_refs, w_refs = take(n_a), take(n_a), take(n_w)
    e_refs, ex_refs, v_refs = take(n_e), take(n_ex), take(n_v)
    o_refs, ox_refs, wbf_refs = take(n_out), take(n_out), take(n_w)

    def run(act_refs, ein_refs, vec_vals, out_refs):
        accs = [jnp.dot(act_refs[a_of_w[i]][...], wbf_refs[i][...], preferred_element_type=F32)
                for i in range(n_w)]
        outs = epilogue(accs, [e[...] for e in ein_refs], vec_vals)
        for o_ref, val in zip(out_refs, outs):
            o_ref[...] = val.astype(o_ref.dtype)

    @pl.when(pl.program_id(1) == 0)
    def _():
        for w_ref, wbf_ref in zip(w_refs, wbf_refs):
            wbf_ref[...] = w_ref[...].astype(BF16)
        run(ax_refs, ex_refs, [], ox_refs)

    run(a_refs, e_refs, [v[0] for v in v_refs], o_refs)


def _matmul(acts, acts_x, weights, a_of_w, w_col_off, n, tn, tm, epilogue, out_dtypes,
            eins=(), e_col_off=(), eins_x=(), ex_col_off=(), vecs=(), rows_per_seq=None, name="mm"):
    m = acts[0].shape[0]
    mx = acts_x[0].shape[0]
    in_specs = []
    for a in acts:
        in_specs.append(pl.BlockSpec((tm, a.shape[1]), lambda j, i: (i, 0)))
    for a in acts_x:
        in_specs.append(pl.BlockSpec((mx, a.shape[1]), lambda j, i: (0, 0)))
    for w, off in zip(weights, w_col_off):
        in_specs.append(pl.BlockSpec((w.shape[0], tn), lambda j, i, o=off // tn: (0, o + j)))
    for e, off in zip(eins, e_col_off):
        in_specs.append(pl.BlockSpec((tm, tn), lambda j, i, o=off // tn: (i, o + j)))
    for e, off in zip(eins_x, ex_col_off):
        in_specs.append(pl.BlockSpec((mx, tn), lambda j, i, o=off // tn: (0, o + j)))
    for v in vecs:
        in_specs.append(pl.BlockSpec((1, 1, tn), lambda j, i: ((i * tm) // rows_per_seq, 0, j)))
    kern = functools.partial(_mm_kernel, n_a=len(acts), a_of_w=tuple(a_of_w), n_e=len(eins),
                             n_ex=len(eins_x), n_v=len(vecs), n_out=len(out_dtypes), epilogue=epilogue)
    outs = pl.pallas_call(
        kern,
        grid=(n // tn, m // tm),
        in_specs=in_specs,
        out_specs=([pl.BlockSpec((tm, tn), lambda j, i: (i, j)) for _ in out_dtypes]
                   + [pl.BlockSpec((mx, tn), lambda j, i: (0, j)) for _ in out_dtypes]),
        out_shape=([jax.ShapeDtypeStruct((m, n), dt) for dt in out_dtypes]
                   + [jax.ShapeDtypeStruct((mx, n), dt) for dt in out_dtypes]),
        scratch_shapes=[pltpu.VMEM((w.shape[0], tn), BF16) for w in weights],
        compiler_params=_params(2),
        name=name,
    )(*acts, *acts_x, *weights, *eins, *eins_x, *vecs)
    return outs[:len(out_dtypes)], outs[len(out_dtypes):]


def _ep_plain(accs, eins, vecs):
    return [accs[0]]


def _ep_merge(accs, eins, vecs):
    return [jax.nn.sigmoid(eins[0]) * accs[0] + jax.nn.sigmoid(eins[1]) * accs[1]]


def _ep_residual(accs, eins, vecs):
    gate = vecs[0] if vecs else eins[1]
    return [eins[0] + gate * accs[0]]


def _ep_swiglu(accs, eins, vecs):
    g = accs[0]
    return [(g * jax.nn.sigmoid(g)) * accs[1]]


def _bias_source_index():
    m = np.arange(BIAS_W)
    rel = np.where(m < K_WIN, BAND_LEFT - m, MAX_REL)
    return np.clip(rel, -(CHUNK - 1), MAX_REL) + (CHUNK - 1)


def _toeplitz_bias(src_row, n_rows):
    t = jnp.broadcast_to(src_row, (n_rows, BIAS_W))
    ri = lax.broadcasted_iota(jnp.int32, (n_rows, BIAS_W), 0)
    for bit in range(max(1, (n_rows - 1).bit_length())):
        t = jnp.where(((ri >> bit) & 1) == 1, pltpu.roll(t, 1 << bit, axis=1), t)
    return t[:, :K_WIN]


def _softmax_pv(s, vw):
    mx = jnp.max(s, axis=-1, keepdims=True)
    e = jnp.exp(s - mx)
    l = jnp.sum(e, axis=-1, keepdims=True)
    pv = jnp.dot(e.astype(BF16), vw, preferred_element_type=F32)
    return pv / l


def _prompt_bias_source_index():
    m = np.arange(P_SRC_W)
    rel = np.where(m < Q_BLK, MAX_REL, m - P_SRC_W + BAND_LEFT)
    return np.clip(rel, -(CHUNK - 1), MAX_REL) + (CHUNK - 1)


def _attn_prompt_kernel(src_ref, mask_ref, q_ref, k_ref, v_ref, o_ref, kpad, vtpad, bias_scr, st_scr,
                        *, t):
    @pl.when(pl.program_id(1) == 0)
    def _():
        r8 = jnp.broadcast_to(src_ref[0], (SUBLANES, P_SRC_W))
        ri = lax.broadcasted_iota(jnp.int32, (SUBLANES, P_SRC_W), 0)
        for bit in range(3):
            r8 = jnp.where(((ri >> bit) & 1) == 1, pltpu.roll(r8, 1 << bit, axis=1), r8)
        ext = jnp.concatenate([r8, r8[:, :Q_BLK]], axis=1)
        ext_w = P_SRC_W + Q_BLK
        for t_off in range(0, LANES, SUBLANES):
            rot = ext if t_off == 0 else pltpu.roll(ext, ext_w - t_off, axis=1)
            for a in range(P_WIN // SUBLANES):
                start = (P_SRC_W - SUBLANES * a) % P_SRC_W
                if start % LANES != t_off:
                    continue
                rows = slice(SUBLANES * a, SUBLANES * (a + 1))
                col = start - t_off
                bias_scr[rows, :] = (rot[:, col:col + Q_BLK] + mask_ref[rows, :]) * LOG2_E

    kpad[0:BAND_LEFT, :] = jnp.zeros((BAND_LEFT, HEAD_DIM), BF16)
    vtpad[:, 0:BAND_LEFT] = jnp.zeros((HEAD_DIM, BAND_LEFT), BF16)
    kpad[BAND_LEFT:, :] = k_ref[...].astype(BF16)
    for c in range(t // T_XPOSE):
        rows = slice(c * T_XPOSE, (c + 1) * T_XPOSE)
        vtpad[:, BAND_LEFT + c * T_XPOSE:BAND_LEFT + (c + 1) * T_XPOSE] = v_ref[rows, :].T.astype(BF16)
    row = lax.broadcasted_iota(jnp.int32, (P_WIN, Q_BLK), 0)
    scale = HEAD_DIM ** -0.5
    n_blk = t // Q_BLK
    half = P_WIN // 2

    def scores(blk, slot):
        r0 = pl.multiple_of(blk * Q_BLK, Q_BLK)
        q = q_ref[pl.ds(r0, Q_BLK), :]
        for hh in range(2):
            kw = kpad[pl.ds(pl.multiple_of(r0 + hh * half, LANES), half), :]
            st_scr[slot, hh * half:(hh + 1) * half, :] = lax.dot_general(
                kw, q, (((1,), (1,)), ((), ())), preferred_element_type=F32)

    def finish(blk, slot, clip_start):
        r0 = pl.multiple_of(blk * Q_BLK, Q_BLK)
        st = st_scr[slot] * (scale * LOG2_E) + bias_scr[...]
        if clip_start:
            st = jnp.where(row >= BAND_LEFT - blk * Q_BLK, st, NEG_INF)
        mx = jnp.max(st, axis=0, keepdims=True)
        e = jnp.exp2(st - mx)
        l = jnp.sum(e, axis=0, keepdims=True)
        vtw = vtpad[:, pl.ds(r0, P_WIN)]
        ot = jnp.dot(vtw, e.astype(BF16), preferred_element_type=F32) / l
        o_ref[pl.ds(r0, Q_BLK), :] = ot.T.astype(o_ref.dtype)

    def pair(j, carry, *, clip_start):
        scores(2 * j + 1, 1)
        finish(2 * j, 0, clip_start)
        scores(jnp.minimum(2 * j + 2, n_blk - 1), 0)
        finish(2 * j + 1, 1, clip_start)
        return carry

    assert n_blk % 2 == 0 and BAND_LEFT == 2 * Q_BLK
    scores(0, 0)
    pair(0, 0, clip_start=True)
    lax.fori_loop(1, n_blk // 2, functools.partial(pair, clip_start=False), 0, unroll=True)


def _prompt_band_mask():
    j = np.arange(P_WIN)[:, None] // CHUNK
    i = np.arange(Q_BLK)[None, :] // CHUNK
    d = i + LEFT_CHUNKS - j
    return np.where((d >= 0) & (d <= LEFT_CHUNKS), 0.0, NEG_INF).astype(np.float32)


def _attn_prompt(src, q, zr, b, t):
    assert t % Q_BLK == 0 and t % T_XPOSE == 0
    mask = jnp.asarray(_prompt_band_mask())
    kern = functools.partial(_attn_prompt_kernel, t=t)
    return pl.pallas_call(
        kern,
        grid=(N_HEADS, b),
        in_specs=[pl.BlockSpec((1, 1, P_SRC_W), lambda h, i: (h, 0, 0)),
                  pl.BlockSpec((P_WIN, Q_BLK), lambda h, i: (0, 0)),
                  pl.BlockSpec((t, HEAD_DIM), lambda h, i: (i, h)),
                  pl.BlockSpec((t, HEAD_DIM), lambda h, i: (i, OFF_K // HEAD_DIM + h)),
                  pl.BlockSpec((t, HEAD_DIM), lambda h, i: (i, OFF_V // HEAD_DIM + h))],
        out_specs=pl.BlockSpec((t, HEAD_DIM), lambda h, i: (i, h)),
        out_shape=jax.ShapeDtypeStruct((b * t, D_ATTN), BF16),
        scratch_shapes=[pltpu.VMEM((BAND_LEFT + t, HEAD_DIM), BF16),
                        pltpu.VMEM((HEAD_DIM, BAND_LEFT + t), BF16),
                        pltpu.VMEM((P_WIN, Q_BLK), F32),
                        pltpu.VMEM((2, P_WIN, Q_BLK), F32)],
        compiler_params=_params(2),
        name="attn_prompt",
    )(src, mask, q, zr, zr)


def _attn_sample_kernel(src_ref, mask_ref, q_ref, kn_ref, vn_ref, ck_ref, cv_ref, o_ref, kall, vall,
                        *, t, r):
    for h in range(N_HEADS):
        cs = slice(h * HEAD_DIM, (h + 1) * HEAD_DIM)
        kall[0:r, cs] = ck_ref[0, :, h, :].astype(BF16)
        vall[0:r, cs] = cv_ref[0, :, h, :].astype(BF16)
    kall[r:r + t, :] = kn_ref[...].astype(BF16)
    vall[r:r + t, :] = vn_ref[...].astype(BF16)
    zeros = jnp.zeros((K_WIN - r - t, D_ATTN), BF16)
    kall[r + t:, :] = zeros
    vall[r + t:, :] = zeros
    scale = HEAD_DIM ** -0.5
    for h in range(N_HEADS):
        cs = slice(h * HEAD_DIM, (h + 1) * HEAD_DIM)
        bias = _toeplitz_bias(src_ref[h:h + 1, :], t) + mask_ref[...]
        s = lax.dot_general(q_ref[:, cs], kall[:, cs], (((1,), (1,)), ((), ())),
                            preferred_element_type=F32)
        s = s * scale + bias
        o_ref[:, cs] = _softmax_pv(s, vall[:, cs]).astype(o_ref.dtype)


def _sample_mask(t, r):
    q_pos = PAST_LEN + np.arange(t)[:, None]
    j = np.arange(K_WIN)[None, :]
    k_pos = PAST_LEN - r + j
    d = q_pos // CHUNK - k_pos // CHUNK
    valid = (j < r + t) & (k_pos >= 0) & (d >= 0) & (d <= LEFT_CHUNKS)
    return np.where(valid, 0.0, NEG_INF).astype(np.float32)


def _attn_sample(src, q, zr, cache_k, cache_v, b, t):
    r = cache_k.shape[1]
    assert r == BAND_LEFT and r + t <= K_WIN and t % 16 == 0
    mask = jnp.asarray(_sample_mask(t, r))
    kern = functools.partial(_attn_sample_kernel, t=t, r=r)
    return pl.pallas_call(
        kern,
        grid=(b,),
        in_specs=[pl.BlockSpec((N_HEADS, BIAS_W), lambda i: (0, 0)),
                  pl.BlockSpec((t, K_WIN), lambda i: (0, 0)),
                  pl.BlockSpec((t, D_ATTN), lambda i: (i, 0)),
                  pl.BlockSpec((t, D_ATTN), lambda i: (i, OFF_K // D_ATTN)),
                  pl.BlockSpec((t, D_ATTN), lambda i: (i, OFF_V // D_ATTN)),
                  pl.BlockSpec((1, r, N_HEADS, HEAD_DIM), lambda i: (i, 0, 0, 0)),
                  pl.BlockSpec((1, r, N_HEADS, HEAD_DIM), lambda i: (i, 0, 0, 0))],
        out_specs=pl.BlockSpec((t, D_ATTN), lambda i: (i, 0)),
        out_shape=jax.ShapeDtypeStruct((b * t, D_ATTN), BF16),
        scratch_shapes=[pltpu.VMEM((K_WIN, D_ATTN), BF16),
                        pltpu.VMEM((K_WIN, D_ATTN), BF16)],
        compiler_params=_params(1),
        name="attn_sample",
    )(src, mask, q, zr, zr, cache_k, cache_v)


def _softplus(x):
    return jnp.maximum(x, 0.0) + jnp.log1p(jnp.exp(-jnp.abs(x)))


def _rglru_kernel(u_ref, g_ref, cw_ref, cb_ref, wa_ref, ba_ref, wx_ref, bx_ref, lam_ref, cinit_ref,
                  h0_ref, y_ref, hl_ref, cout_ref, ext, a_scr, b_scr, h_scr, hcarry, *, tt, cb):
    ti = pl.program_id(2)
    pad = SUBLANES

    @pl.when(ti == 0)
    def _():
        ext[0:pad, :] = cinit_ref[0]
        hcarry[...] = h0_ref[0]

    @pl.when(ti > 0)
    def _():
        ext[0:pad, :] = ext[tt:tt + pad, :]

    u = u_ref[...]
    ext[pad:pad + tt, :] = u
    cw = 0.5 * cw_ref[...]
    xh = 0.5 * cb_ref[...] + u * cw[CONV_WIDTH - 1:CONV_WIDTH, :]
    for j in range(CONV_WIDTH - 1):
        back = CONV_WIDTH - 1 - j
        xh = xh + ext[pad - back:pad - back + tt, :] * cw[j:j + 1, :]

    xhb = xh.astype(BF16)
    r_parts, i_parts = [], []
    for n in range(cb // RG_BLOCK_DIM):
        cs = slice(n * RG_BLOCK_DIM, (n + 1) * RG_BLOCK_DIM)
        r_parts.append(jnp.dot(xhb[:, cs], wa_ref[n].astype(BF16), preferred_element_type=F32))
        i_parts.append(jnp.dot(xhb[:, cs], wx_ref[n].astype(BF16), preferred_element_type=F32))
    tr = jnp.tanh(jnp.concatenate(r_parts, axis=1) + 0.5 * ba_ref[...])
    ti = jnp.tanh(jnp.concatenate(i_parts, axis=1) + 0.5 * bx_ref[...])
    log_a = (1.0 + tr) * ((-0.5 * RG_C) * _softplus(-lam_ref[...]))
    a = jnp.exp(log_a)
    v = jnp.tanh(log_a) * (-1.0 - a * a)
    root = jnp.where(v > 0.0, v * lax.rsqrt(v), 0.0)
    b = root * ((1.0 + ti) * xh)

    groups = tt // SUBLANES
    a3 = a.reshape(groups, SUBLANES, cb)
    b3 = b.reshape(groups, SUBLANES, cb)
    si = lax.broadcasted_iota(jnp.int32, (groups, SUBLANES, cb), 1)
    for k in (1, 2, 4):
        keep = si >= k
        a_prev = pltpu.roll(a3, k, axis=1)
        b_prev = pltpu.roll(b3, k, axis=1)
        b3 = jnp.where(keep, a3 * b_prev + b3, b3)
        a3 = jnp.where(keep, a3 * a_prev, a3)
    a_scr[...] = a3
    b_scr[...] = b3

    def step(gi, h_prev):
        hg = a_scr[gi] * h_prev + b_scr[gi]
        h_scr[gi] = hg
        return hg[SUBLANES - 1:SUBLANES, :]

    h_last = lax.fori_loop(0, groups, step, hcarry[...], unroll=min(groups, 8))
    hcarry[...] = h_last
    h = h_scr[...].reshape(tt, cb)
    y_ref[...] = (h * jax.nn.gelu(g_ref[...])).astype(y_ref.dtype)
    hl_ref[0] = h_last
    cout_ref[0] = ext[tt:tt + pad, :]


def _rglru(zr, b, t, tt, cb, conv_w, conv_b, w_rg_a, b_rg_a, w_rg_x, b_rg_x, lam, cinit, h0):
    assert t % tt == 0 and tt % SUBLANES == 0 and D_RNN % cb == 0 and cb % RG_BLOCK_DIM == 0
    nt = t // tt
    nb = cb // RG_BLOCK_DIM
    vec = lambda: pl.BlockSpec((1, cb), lambda i, j, k: (0, j))
    kern = functools.partial(_rglru_kernel, tt=tt, cb=cb)
    return pl.pallas_call(
        kern,
        grid=(b, D_RNN // cb, nt),
        in_specs=[pl.BlockSpec((tt, cb), lambda i, j, k: (i * nt + k, OFF_U // cb + j)),
                  pl.BlockSpec((tt, cb), lambda i, j, k: (i * nt + k, OFF_G // cb + j)),
                  pl.BlockSpec((CONV_WIDTH, cb), lambda i, j, k: (0, j)),
                  vec(),
                  pl.BlockSpec((nb, RG_BLOCK_DIM, RG_BLOCK_DIM), lambda i, j, k: (j, 0, 0)),
                  vec(),
                  pl.BlockSpec((nb, RG_BLOCK_DIM, RG_BLOCK_DIM), lambda i, j, k: (j, 0, 0)),
                  vec(),
                  vec(),
                  pl.BlockSpec((1, SUBLANES, cb), lambda i, j, k: (i, 0, j)),
                  pl.BlockSpec((1, 1, cb), lambda i, j, k: (i, 0, j))],
        out_specs=[pl.BlockSpec((tt, cb), lambda i, j, k: (i * nt + k, j)),
                   pl.BlockSpec((1, 1, cb), lambda i, j, k: (i, 0, j)),
                   pl.BlockSpec((1, SUBLANES, cb), lambda i, j, k: (i, 0, j))],
        out_shape=[jax.ShapeDtypeStruct((b * t, D_RNN), BF16),
                   jax.ShapeDtypeStruct((b, 1, D_RNN), F32),
                   jax.ShapeDtypeStruct((b, SUBLANES, D_RNN), F32)],
        scratch_shapes=[pltpu.VMEM((tt + SUBLANES, cb), F32),
                        pltpu.VMEM((tt // SUBLANES, SUBLANES, cb), F32),
                        pltpu.VMEM((tt // SUBLANES, SUBLANES, cb), F32),
                        pltpu.VMEM((tt // SUBLANES, SUBLANES, cb), F32),
                        pltpu.VMEM((1, cb), F32)],
        compiler_params=_params(3),
        name="rglru",
    )(zr, zr, conv_w, conv_b, w_rg_a, b_rg_a, w_rg_x, b_rg_x, lam, cinit, h0)


def _layer(xp3, xs3, mods_p, mods_s, p, src_p, src_s, cache_k, cache_v, conv_s, h0_s, tm):
    bp, tp, d = xp3.shape
    bs, ts, _ = xs3.shape
    mp, ms = bp * tp, bs * ts
    assert tp % tm == 0
    sh1, sc1, g1, sh2, sc2, g2 = mods_p
    sh1s, sc1s, g1s, sh2s, sc2s, g2s = mods_s
    xp, xs = xp3.reshape(mp, d), xs3.reshape(ms, d)
    rows_s = lambda gate: jnp.broadcast_to(gate, (bs, ts, d)).reshape(ms, d)

    hp = _prenorm(xp3, sc1, sh1, tm).reshape(mp, d)
    hs = _prenorm(xs3, sc1s, sh1s, ts).reshape(ms, d)
    (qp,), (qs,) = _matmul([hp], [hs], [p['w_in']], [0], [0], D_ATTN, 1024, tm, _ep_plain, [BF16],
                           name="in_proj_q")
    (zp,), (zs,) = _matmul([hp], [hs], [p['w_in']], [0], [D_ATTN], 6 * 2048, 1024, tm, _ep_plain, [F32],
                           name="in_proj_rest")

    op = _attn_prompt(src_p, qp, zp, bp, tp)
    os_ = _attn_sample(src_s, qs, zs, cache_k, cache_v, bs, ts)
    rg = (p['conv_w'], p['conv_b'], p['w_rg_a'], p['b_rg_a'], p['w_rg_x'], p['b_rg_x'], p['rg_lambda'])
    yp, hlp, cop = _rglru(zp, bp, tp, 512, 512, *rg, jnp.zeros((bp, SUBLANES, D_RNN), F32),
                          jnp.zeros((bp, 1, D_RNN), F32))
    ys, hls, cos = _rglru(zs, bs, ts, ts, D_RNN, *rg, conv_s, h0_s)

    (mgp,), (mgs,) = _matmul([op, yp], [os_, ys], [p['w_proj_attn'], p['w_proj_rnn']], [0, 1], [0, 0],
                             D_MODEL, 512, tm, _ep_merge, [BF16], eins=[zp, zp],
                             e_col_off=[OFF_GA, OFF_GR], eins_x=[zs, zs], ex_col_off=[OFF_GA, OFF_GR],
                             name="branch_proj")
    (x1p,), (x1s,) = _matmul([mgp], [mgs], [p['w_out']], [0], [0], D_MODEL, 1024, tm, _ep_residual, [F32],
                             eins=[xp], e_col_off=[0], eins_x=[xs, rows_s(g1s)], ex_col_off=[0, 0],
                             vecs=[g1], rows_per_seq=tp, name="out_proj")

    h2p = _prenorm(x1p.reshape(bp, tp, d), sc2, sh2, tm).reshape(mp, d)
    h2s = _prenorm(x1s.reshape(bs, ts, d), sc2s, sh2s, ts).reshape(ms, d)
    d_ff = p['w_ffn_gate'].shape[1]
    (actp,), (acts,) = _matmul([h2p], [h2s], [p['w_ffn_gate'], p['w_ffn_up']], [0, 0], [0, 0], d_ff, 512,
                               tm, _ep_swiglu, [BF16], name="ffn_up")
    (xop,), (xos,) = _matmul([actp], [acts], [p['w_ffn_down']], [0], [0], D_MODEL, 512, tm, _ep_residual,
                             [F32], eins=[x1p], e_col_off=[0], eins_x=[x1s, rows_s(g2s)],
                             ex_col_off=[0, 0], vecs=[g2], rows_per_seq=tp, name="ffn_down")
    y_p = _final_norm(xop, p['final_norm_g'], tm).reshape(bp, tp, d)
    y_s = _final_norm(xos, p['final_norm_g'], ms).reshape(bs, ts, d)

    def states(z, b, t, h_last, cout):
        keep = min(BAND_LEFT, t)
        z3 = z.reshape(b, t, z.shape[1])
        new_k = z3[:, t - keep:, OFF_K:OFF_K + D_ATTN].reshape(b, keep, N_HEADS, HEAD_DIM)
        new_v = z3[:, t - keep:, OFF_V:OFF_V + D_ATTN].reshape(b, keep, N_HEADS, HEAD_DIM)
        return new_k, new_v, h_last.reshape(b, D_RNN), cout[:, SUBLANES - (CONV_WIDTH - 1):, :]

    return (y_p,) + states(zp, bp, tp, hlp, cop), (y_s,) + states(zs, bs, ts, hls, cos)


def kernel(x_prompt, x_sample, c_prompt, c_sample, cache_attn_k, cache_attn_v, state_rglru_h,
           state_rglru_conv, w_mod, b_mod, w_in, rel_bias, conv_w, conv_b, w_rg_a, b_rg_a, w_rg_x,
           b_rg_x, rg_lambda, w_proj_attn, w_proj_rnn, w_out, w_ffn_gate, w_ffn_up, w_ffn_down,
           final_norm_g):
    assert w_mod.shape[0] == 1, "single-layer step"
    bp, tp, d = x_prompt.shape
    bs, ts, _ = x_sample.shape
    p = {'w_in': w_in[0], 'conv_w': conv_w[0], 'conv_b': conv_b[0][None], 'w_rg_a': w_rg_a[0],
         'b_rg_a': b_rg_a[0][None], 'w_rg_x': w_rg_x[0], 'b_rg_x': b_rg_x[0][None],
         'rg_lambda': rg_lambda[0][None], 'w_proj_attn': w_proj_attn[0], 'w_proj_rnn': w_proj_rnn[0],
         'w_out': w_out[0], 'w_ffn_gate': w_ffn_gate[0], 'w_ffn_up': w_ffn_up[0],
         'w_ffn_down': w_ffn_down[0], 'final_norm_g': final_norm_g[None]}

    n_seq = bp + bs
    rows = -(-n_seq // 16) * 16
    c_all = jnp.concatenate([c_prompt, c_sample, jnp.zeros((rows - n_seq, d), F32)], axis=0)
    mod = _modulation(c_all, w_mod[0], b_mod[0][None])
    mods_p = [mod[:bp, i * d:(i + 1) * d][:, None, :] for i in range(6)]
    mods_s = [mod[bp:n_seq, i * d:(i + 1) * d][:, None, :] for i in range(6)]

    src = jnp.take(rel_bias[0], jnp.asarray(_bias_source_index()), axis=1)
    src_p = jnp.take(rel_bias[0], jnp.asarray(_prompt_bias_source_index()), axis=1)

    conv_state = jnp.pad(state_rglru_conv[0], ((0, 0), (SUBLANES - (CONV_WIDTH - 1), 0), (0, 0)))
    (yp, k1, v1, h1, b1), (ys, k2, v2, h2, b2) = _layer(
        x_prompt, x_sample, mods_p, mods_s, p, src_p[:, None, :], src, cache_attn_k[0], cache_attn_v[0],
        conv_state, state_rglru_h[0][:, None, :], tm=512)

    return (yp, ys, k1[None], v1[None], h1[None], b1[None], k2[None], v2[None], h2[None], b2[None])
```

```python
import functools

import numpy as np
import jax
import jax.numpy as jnp
from jax import lax
from jax.experimental import pallas as pl
from jax.experimental.pallas import tpu as pltpu

F32 = jnp.float32
BF16 = jnp.bfloat16

D_MODEL = 2048
CHUNK = 64
LEFT_CHUNKS = 8
BAND_LEFT = LEFT_CHUNKS * CHUNK
N_HEADS = 16
HEAD_DIM = 128
D_ATTN = N_HEADS * HEAD_DIM
MAX_REL = 256
N_REL = CHUNK + MAX_REL
D_RNN = 2048
RG_BLOCKS = 16
RG_BLOCK_DIM = D_RNN // RG_BLOCKS
CONV_WIDTH = 4
RG_C = 8.0
PAST_LEN = 1024
NORM_EPS = 1e-6
NEG_INF = -1e30
LOG2_E = 1.4426950408889634

LANES = 128
SUBLANES = 8
VMEM_LIMIT = 56 * 1024 * 1024

OFF_K, OFF_V, OFF_U, OFF_G, OFF_GA, OFF_GR = (i * 2048 for i in range(6))

Q_PAIR = 2 * CHUNK
K_WIN = BAND_LEFT + Q_PAIR
BIAS_W = K_WIN + LANES
Q_BLK = 4 * CHUNK
P_WIN = BAND_LEFT + Q_BLK
P_SRC_W = P_WIN + Q_BLK
T_XPOSE = 512


def _params(n_axes):
    return pltpu.CompilerParams(dimension_semantics=("arbitrary",) * n_axes,
                                vmem_limit_bytes=VMEM_LIMIT)


def _mod_kernel(c_ref, w_ref, b_ref, o_ref):
    c = c_ref[...]
    s = (c * jax.nn.sigmoid(c)).astype(BF16)
    o_ref[...] = jnp.dot(s, w_ref[...].astype(BF16), preferred_element_type=F32) + b_ref[...]


def _modulation(c_all, w_mod, b_mod, tn=1024):
    rows, d = c_all.shape
    n = w_mod.shape[1]
    return pl.pallas_call(
        _mod_kernel,
        grid=(n // tn,),
        in_specs=[pl.BlockSpec((rows, d), lambda j: (0, 0)),
                  pl.BlockSpec((d, tn), lambda j: (0, j)),
                  pl.BlockSpec((1, tn), lambda j: (0, j))],
        out_specs=pl.BlockSpec((rows, tn), lambda j: (0, j)),
        out_shape=jax.ShapeDtypeStruct((rows, n), F32),
        compiler_params=_params(1),
        name="adaln_mod",
    )(c_all, w_mod, b_mod)


def _rms(xf):
    return xf * lax.rsqrt(jnp.mean(xf * xf, axis=-1, keepdims=True) + NORM_EPS)


def _prenorm_kernel(x_ref, sc_ref, sh_ref, o_ref):
    o_ref[0] = (_rms(x_ref[0]) * (1.0 + sc_ref[0]) + sh_ref[0]).astype(o_ref.dtype)


def _prenorm(x3, sc, sh, tm):
    b, t, d = x3.shape
    return pl.pallas_call(
        _prenorm_kernel,
        grid=(b, t // tm),
        in_specs=[pl.BlockSpec((1, tm, d), lambda i, j: (i, j, 0)),
                  pl.BlockSpec((1, 1, d), lambda i, j: (i, 0, 0)),
                  pl.BlockSpec((1, 1, d), lambda i, j: (i, 0, 0))],
        out_specs=pl.BlockSpec((1, tm, d), lambda i, j: (i, j, 0)),
        out_shape=jax.ShapeDtypeStruct((b, t, d), BF16),
        compiler_params=_params(2),
        name="prenorm",
    )(x3, sc, sh)


def _final_norm_kernel(x_ref, g_ref, o_ref):
    o_ref[...] = _rms(x_ref[...]) * g_ref[...]


def _final_norm(x2, gain, tm):
    m, d = x2.shape
    return pl.pallas_call(
        _final_norm_kernel,
        grid=(m // tm,),
        in_specs=[pl.BlockSpec((tm, d), lambda i: (i, 0)),
                  pl.BlockSpec((1, d), lambda i: (0, 0))],
        out_specs=pl.BlockSpec((tm, d), lambda i: (i, 0)),
        out_shape=jax.ShapeDtypeStruct((m, d), F32),
        compiler_params=_params(1),
        name="final_norm",
    )(x2, gain)


def _mm_kernel(*refs, n_a, a_of_w, n_e, n_ex, n_v, n_out, epilogue):
    it = iter(refs)
    take = lambda k: [next(it) for _ in range(k)]
    n_w = len(a_of_w)
    a_refs, ax_refs, w_refs = take(n_a), take(n_a), take(n_w)
    e_refs, ex_refs, v_refs = take(n_e), take(n_ex), take(n_v)
    o_refs, ox_refs, wbf_refs = take(n_out), take(n_out), take(n_w)

    def run(act_refs, ein_refs, vec_vals, out_refs):
        accs = [jnp.dot(act_refs[a_of_w[i]][...], wbf_refs[i][...], preferred_element_type=F32)
                for i in range(n_w)]
        outs = epilogue(accs, [e[...] for e in ein_refs], vec_vals)
        for o_ref, val in zip(out_refs, outs):
            o_ref[...] = val.astype(o_ref.dtype)

    @pl.when(pl.program_id(1) == 0)
    def _():
        for w_ref, wbf_ref in zip(w_refs, wbf_refs):
            wbf_ref[...] = w_ref[...].astype(BF16)
        run(ax_refs, ex_refs, [], ox_refs)

    run(a_refs, e_refs, [v[0] for v in v_refs], o_refs)


def _matmul(acts, acts_x, weights, a_of_w, w_col_off, n, tn, tm, epilogue, out_dtypes,
            eins=(), e_col_off=(), eins_x=(), ex_col_off=(), vecs=(), rows_per_seq=None, name="mm"):
    m = acts[0].shape[0]
    mx = acts_x[0].shape[0]
    in_specs = []
    for a in acts:
        in_specs.append(pl.BlockSpec((tm, a.shape[1]), lambda j, i: (i, 0)))
    for a in acts_x:
        in_specs.append(pl.BlockSpec((mx, a.shape[1]), lambda j, i: (0, 0)))
    for w, off in zip(weights, w_col_off):
        in_specs.append(pl.BlockSpec((w.shape[0], tn), lambda j, i, o=off // tn: (0, o + j)))
    for e, off in zip(eins, e_col_off):
        in_specs.append(pl.BlockSpec((tm, tn), lambda j, i, o=off // tn: (i, o + j)))
    for e, off in zip(eins_x, ex_col_off):
        in_specs.append(pl.BlockSpec((mx, tn), lambda j, i, o=off // tn: (0, o + j)))
    for v in vecs:
        in_specs.append(pl.BlockSpec((1, 1, tn), lambda j, i: ((i * tm) // rows_per_seq, 0, j)))
    kern = functools.partial(_mm_kernel, n_a=len(acts), a_of_w=tuple(a_of_w), n_e=len(eins),
                             n_ex=len(eins_x), n_v=len(vecs), n_out=len(out_dtypes), epilogue=epilogue)
    outs = pl.pallas_call(
        kern,
        grid=(n // tn, m // tm),
        in_specs=in_specs,
        out_specs=([pl.BlockSpec((tm, tn), lambda j, i: (i, j)) for _ in out_dtypes]
                   + [pl.BlockSpec((mx, tn), lambda j, i: (0, j)) for _ in out_dtypes]),
        out_shape=([jax.ShapeDtypeStruct((m, n), dt) for dt in out_dtypes]
                   + [jax.ShapeDtypeStruct((mx, n), dt) for dt in out_dtypes]),
        scratch_shapes=[pltpu.VMEM((w.shape[0], tn), BF16) for w in weights],
        compiler_params=_params(2),
        name=name,
    )(*acts, *acts_x, *weights, *eins, *eins_x, *vecs)
    return outs[:len(out_dtypes)], outs[len(out_dtypes):]


def _ep_plain(accs, eins, vecs):
    return [accs[0]]


def _ep_merge(accs, eins, vecs):
    return [jax.nn.sigmoid(eins[0]) * accs[0] + jax.nn.sigmoid(eins[1]) * accs[1]]


def _ep_residual(accs, eins, vecs):
    gate = vecs[0] if vecs else eins[1]
    return [eins[0] + gate * accs[0]]


def _ep_swiglu(accs, eins, vecs):
    g = accs[0]
    return [(g * jax.nn.sigmoid(g)) * accs[1]]


def _bias_source_index():
    m = np.arange(BIAS_W)
    rel = np.where(m < K_WIN, BAND_LEFT - m, MAX_REL)
    return np.clip(rel, -(CHUNK - 1), MAX_REL) + (CHUNK - 1)


def _toeplitz_bias(src_row, n_rows):
    t = jnp.broadcast_to(src_row, (n_rows, BIAS_W))
    ri = lax.broadcasted_iota(jnp.int32, (n_rows, BIAS_W), 0)
    for bit in range(max(1, (n_rows - 1).bit_length())):
        t = jnp.where(((ri >> bit) & 1) == 1, pltpu.roll(t, 1 << bit, axis=1), t)
    return t[:, :K_WIN]


def _softmax_pv(s, vw):
    mx = jnp.max(s, axis=-1, keepdims=True)
    e = jnp.exp(s - mx)
    l = jnp.sum(e, axis=-1, keepdims=True)
    pv = jnp.dot(e.astype(BF16), vw, preferred_element_type=F32)
    return pv / l


def _prompt_bias_source_index():
    m = np.arange(P_SRC_W)
    rel = np.where(m < Q_BLK, MAX_REL, m - P_SRC_W + BAND_LEFT)
    return np.clip(rel, -(CHUNK - 1), MAX_REL) + (CHUNK - 1)


def _attn_prompt_kernel(src_ref, mask_ref, q_ref, k_ref, v_ref, o_ref, kpad, vtpad, bias_scr, st_scr,
                        *, t):
    @pl.when(pl.program_id(1) == 0)
    def _():
        r8 = jnp.broadcast_to(src_ref[0], (SUBLANES, P_SRC_W))
        ri = lax.broadcasted_iota(jnp.int32, (SUBLANES, P_SRC_W), 0)
        for bit in range(3):
            r8 = jnp.where(((ri >> bit) & 1) == 1, pltpu.roll(r8, 1 << bit, axis=1), r8)
        ext = jnp.concatenate([r8, r8[:, :Q_BLK]], axis=1)
        ext_w = P_SRC_W + Q_BLK
        for t_off in range(0, LANES, SUBLANES):
            rot = ext if t_off == 0 else pltpu.roll(ext, ext_w - t_off, axis=1)
            for a in range(P_WIN // SUBLANES):
                start = (P_SRC_W - SUBLANES * a) % P_SRC_W
                if start % LANES != t_off:
                    continue
                rows = slice(SUBLANES * a, SUBLANES * (a + 1))
                col = start - t_off
                bias_scr[rows, :] = (rot[:, col:col + Q_BLK] + mask_ref[rows, :]) * LOG2_E

    kpad[0:BAND_LEFT, :] = jnp.zeros((BAND_LEFT, HEAD_DIM), BF16)
    vtpad[:, 0:BAND_LEFT] = jnp.zeros((HEAD_DIM, BAND_LEFT), BF16)
    kpad[BAND_LEFT:, :] = k_ref[...].astype(BF16)
    for c in range(t // T_XPOSE):
        rows = slice(c * T_XPOSE, (c + 1) * T_XPOSE)
        vtpad[:, BAND_LEFT + c * T_XPOSE:BAND_LEFT + (c + 1) * T_XPOSE] = v_ref[rows, :].T.astype(BF16)
    row = lax.broadcasted_iota(jnp.int32, (P_WIN, Q_BLK), 0)
    scale = HEAD_DIM ** -0.5
    n_blk = t // Q_BLK
    half = P_WIN // 2

    def scores(blk, slot):
        r0 = pl.multiple_of(blk * Q_BLK, Q_BLK)
        q = q_ref[pl.ds(r0, Q_BLK), :]
        for hh in range(2):
            kw = kpad[pl.ds(pl.multiple_of(r0 + hh * half, LANES), half), :]
            st_scr[slot, hh * half:(hh + 1) * half, :] = lax.dot_general(
                kw, q, (((1,), (1,)), ((), ())), preferred_element_type=F32)

    def finish(blk, slot, clip_start):
        r0 = pl.multiple_of(blk * Q_BLK, Q_BLK)
        st = st_scr[slot] * (scale * LOG2_E) + bias_scr[...]
        if clip_start:
            st = jnp.where(row >= BAND_LEFT - blk * Q_BLK, st, NEG_INF)
        mx = jnp.max(st, axis=0, keepdims=True)
        e = jnp.exp2(st - mx)
        l = jnp.sum(e, axis=0, keepdims=True)
        vtw = vtpad[:, pl.ds(r0, P_WIN)]
        ot = jnp.dot(vtw, e.astype(BF16), preferred_element_type=F32) / l
        o_ref[pl.ds(r0, Q_BLK), :] = ot.T.astype(o_ref.dtype)

    def pair(j, carry, *, clip_start):
        scores(2 * j + 1, 1)
        finish(2 * j, 0, clip_start)
        scores(jnp.minimum(2 * j + 2, n_blk - 1), 0)
        finish(2 * j + 1, 1, clip_start)
        return carry

    assert n_blk % 2 == 0 and BAND_LEFT == 2 * Q_BLK
    scores(0, 0)
    pair(0, 0, clip_start=True)
    lax.fori_loop(1, n_blk // 2, functools.partial(pair, clip_start=False), 0, unroll=True)


def _prompt_band_mask():
    j = np.arange(P_WIN)[:, None] // CHUNK
    i = np.arange(Q_BLK)[None, :] // CHUNK
    d = i + LEFT_CHUNKS - j
    return np.where((d >= 0) & (d <= LEFT_CHUNKS), 0.0, NEG_INF).astype(np.float32)


def _attn_prompt(src, q, zr, b, t):
    assert t % Q_BLK == 0 and t % T_XPOSE == 0
    mask = jnp.asarray(_prompt_band_mask())
    kern = functools.partial(_attn_prompt_kernel, t=t)
    return pl.pallas_call(
        kern,
        grid=(N_HEADS, b),
        in_specs=[pl.BlockSpec((1, 1, P_SRC_W), lambda h, i: (h, 0, 0)),
                  pl.BlockSpec((P_WIN, Q_BLK), lambda h, i: (0, 0)),
                  pl.BlockSpec((t, HEAD_DIM), lambda h, i: (i, h)),
                  pl.BlockSpec((t, HEAD_DIM), lambda h, i: (i, OFF_K // HEAD_DIM + h)),
                  pl.BlockSpec((t, HEAD_DIM), lambda h, i: (i, OFF_V // HEAD_DIM + h))],
        out_specs=pl.BlockSpec((t, HEAD_DIM), lambda h, i: (i, h)),
        out_shape=jax.ShapeDtypeStruct((b * t, D_ATTN), BF16),
        scratch_shapes=[pltpu.VMEM((BAND_LEFT + t, HEAD_DIM), BF16),
                        pltpu.VMEM((HEAD_DIM, BAND_LEFT + t), BF16),
                        pltpu.VMEM((P_WIN, Q_BLK), F32),
                        pltpu.VMEM((2, P_WIN, Q_BLK), F32)],
        compiler_params=_params(2),
        name="attn_prompt",
    )(src, mask, q, zr, zr)


def _attn_sample_kernel(src_ref, mask_ref, q_ref, kn_ref, vn_ref, ck_hbm, cv_hbm, o_ref, kbuf, vbuf, sems,
                        kwin, vwin, *, t, r):
    seq = pl.program_id(0)

    def head_copies(h, slot):
        return (pltpu.make_async_copy(ck_hbm.at[seq, :, h, :], kbuf.at[slot], sems.at[0, slot]),
                pltpu.make_async_copy(cv_hbm.at[seq, :, h, :], vbuf.at[slot], sems.at[1, slot]))

    for cp in head_copies(0, 0):
        cp.start()
    zeros = jnp.zeros((K_WIN - r - t, HEAD_DIM), BF16)
    kwin[r + t:, :] = zeros
    vwin[r + t:, :] = zeros
    scale = HEAD_DIM ** -0.5
    for h in range(N_HEADS):
        slot = h % 2
        if h + 1 < N_HEADS:
            for cp in head_copies(h + 1, 1 - slot):
                cp.start()
        cs = slice(h * HEAD_DIM, (h + 1) * HEAD_DIM)
        kwin[r:r + t, :] = kn_ref[:, cs].astype(BF16)
        vwin[r:r + t, :] = vn_ref[:, cs].astype(BF16)
        bias = _toeplitz_bias(src_ref[h:h + 1, :], t) + mask_ref[...]
        for cp in head_copies(h, slot):
            cp.wait()
        kwin[0:r, :] = kbuf[slot].astype(BF16)
        vwin[0:r, :] = vbuf[slot].astype(BF16)
        s = lax.dot_general(q_ref[:, cs], kwin[...], (((1,), (1,)), ((), ())),
                            preferred_element_type=F32)
        s = s * scale + bias
        o_ref[:, cs] = _softmax_pv(s, vwin[...]).astype(o_ref.dtype)


def _sample_mask(t, r):
    q_pos = PAST_LEN + np.arange(t)[:, None]
    j = np.arange(K_WIN)[None, :]
    k_pos = PAST_LEN - r + j
    d = q_pos // CHUNK - k_pos // CHUNK
    valid = (j < r + t) & (k_pos >= 0) & (d >= 0) & (d <= LEFT_CHUNKS)
    return np.where(valid, 0.0, NEG_INF).astype(np.float32)


def _attn_sample(src, q, zr, cache_k, cache_v, b, t):
    r = cache_k.shape[1]
    assert r == BAND_LEFT and r + t <= K_WIN and t % 16 == 0
    mask = jnp.asarray(_sample_mask(t, r))
    kern = functools.partial(_attn_sample_kernel, t=t, r=r)
    return pl.pallas_call(
        kern,
        grid=(b,),
        in_specs=[pl.BlockSpec((N_HEADS, BIAS_W), lambda i: (0, 0)),
                  pl.BlockSpec((t, K_WIN), lambda i: (0, 0)),
                  pl.BlockSpec((t, D_ATTN), lambda i: (i, 0)),
                  pl.BlockSpec((t, D_ATTN), lambda i: (i, OFF_K // D_ATTN)),
                  pl.BlockSpec((t, D_ATTN), lambda i: (i, OFF_V // D_ATTN)),
                  pl.BlockSpec(memory_space=pl.ANY),
                  pl.BlockSpec(memory_space=pl.ANY)],
        out_specs=pl.BlockSpec((t, D_ATTN), lambda i: (i, 0)),
        out_shape=jax.ShapeDtypeStruct((b * t, D_ATTN), BF16),
        scratch_shapes=[pltpu.VMEM((2, r, HEAD_DIM), F32),
                        pltpu.VMEM((2, r, HEAD_DIM), F32),
                        pltpu.SemaphoreType.DMA((2, 2)),
                        pltpu.VMEM((K_WIN, HEAD_DIM), BF16),
                        pltpu.VMEM((K_WIN, HEAD_DIM), BF16)],
        compiler_params=_params(1),
        name="attn_sample",
    )(src, mask, q, zr, zr, cache_k, cache_v)


def _softplus(x):
    return jnp.maximum(x, 0.0) + jnp.log1p(jnp.exp(-jnp.abs(x)))


def _rglru_kernel(u_ref, g_ref, cw_ref, cb_ref, wa_ref, ba_ref, wx_ref, bx_ref, lam_ref, cinit_ref,
                  h0_ref, y_ref, hl_ref, cout_ref, ext, a_scr, b_scr, h_scr, hcarry, *, tt, cb):
    ti = pl.program_id(2)
    pad = SUBLANES

    @pl.when(ti == 0)
    def _():
        ext[0:pad, :] = cinit_ref[0]
        hcarry[...] = h0_ref[0]

    @pl.when(ti > 0)
    def _():
        ext[0:pad, :] = ext[tt:tt + pad, :]

    u = u_ref[...]
    ext[pad:pad + tt, :] = u
    cw = 0.5 * cw_ref[...]
    xh = 0.5 * cb_ref[...] + u * cw[CONV_WIDTH - 1:CONV_WIDTH, :]
    for j in range(CONV_WIDTH - 1):
        back = CONV_WIDTH - 1 - j
        xh = xh + ext[pad - back:pad - back + tt, :] * cw[j:j + 1, :]

    xhb = xh.astype(BF16)
    r_parts, i_parts = [], []
    for n in range(cb // RG_BLOCK_DIM):
        cs = slice(n * RG_BLOCK_DIM, (n + 1) * RG_BLOCK_DIM)
        r_parts.append(jnp.dot(xhb[:, cs], wa_ref[n].astype(BF16), preferred_element_type=F32))
        i_parts.append(jnp.dot(xhb[:, cs], wx_ref[n].astype(BF16), preferred_element_type=F32))
    tr = jnp.tanh(jnp.concatenate(r_parts, axis=1) + 0.5 * ba_ref[...])
    ti = jnp.tanh(jnp.concatenate(i_parts, axis=1) + 0.5 * bx_ref[...])
    log_a = (1.0 + tr) * ((-0.5 * RG_C) * _softplus(-lam_ref[...]))
    a = jnp.exp(log_a)
    v = jnp.tanh(log_a) * (-1.0 - a * a)
    root = jnp.where(v > 0.0, v * lax.rsqrt(v), 0.0)
    b = root * ((1.0 + ti) * xh)

    groups = tt // SUBLANES
    a3 = a.reshape(groups, SUBLANES, cb)
    b3 = b.reshape(groups, SUBLANES, cb)
    si = lax.broadcasted_iota(jnp.int32, (groups, SUBLANES, cb), 1)
    for k in (1, 2, 4):
        keep = si >= k
        a_prev = pltpu.roll(a3, k, axis=1)
        b_prev = pltpu.roll(b3, k, axis=1)
        b3 = jnp.where(keep, a3 * b_prev + b3, b3)
        a3 = jnp.where(keep, a3 * a_prev, a3)
    a_scr[...] = a3
    b_scr[...] = b3

    def step(gi, h_prev):
        hg = a_scr[gi] * h_prev + b_scr[gi]
        h_scr[gi] = hg
        return hg[SUBLANES - 1:SUBLANES, :]

    h_last = lax.fori_loop(0, groups, step, hcarry[...], unroll=min(groups, 8))
    hcarry[...] = h_last
    h = h_scr[...].reshape(tt, cb)
    y_ref[...] = (h * jax.nn.gelu(g_ref[...])).astype(y_ref.dtype)
    hl_ref[0] = h_last
    cout_ref[0] = ext[tt:tt + pad, :]


def _rglru(zr, b, t, tt, cb, conv_w, conv_b, w_rg_a, b_rg_a, w_rg_x, b_rg_x, lam, cinit, h0):
    assert t % tt == 0 and tt % SUBLANES == 0 and D_RNN % cb == 0 and cb % RG_BLOCK_DIM == 0
    nt = t // tt
    nb = cb // RG_BLOCK_DIM
    vec = lambda: pl.BlockSpec((1, cb), lambda i, j, k: (0, j))
    kern = functools.partial(_rglru_kernel, tt=tt, cb=cb)
    return pl.pallas_call(
        kern,
        grid=(b, D_RNN // cb, nt),
        in_specs=[pl.BlockSpec((tt, cb), lambda i, j, k: (i * nt + k, OFF_U // cb + j)),
                  pl.BlockSpec((tt, cb), lambda i, j, k: (i * nt + k, OFF_G // cb + j)),
                  pl.BlockSpec((CONV_WIDTH, cb), lambda i, j, k: (0, j)),
                  vec(),
                  pl.BlockSpec((nb, RG_BLOCK_DIM, RG_BLOCK_DIM), lambda i, j, k: (j, 0, 0)),
                  vec(),
                  pl.BlockSpec((nb, RG_BLOCK_DIM, RG_BLOCK_DIM), lambda i, j, k: (j, 0, 0)),
                  vec(),
                  vec(),
                  pl.BlockSpec((1, SUBLANES, cb), lambda i, j, k: (i, 0, j)),
                  pl.BlockSpec((1, 1, cb), lambda i, j, k: (i, 0, j))],
        out_specs=[pl.BlockSpec((tt, cb), lambda i, j, k: (i * nt + k, j)),
                   pl.BlockSpec((1, 1, cb), lambda i, j, k: (i, 0, j)),
                   pl.BlockSpec((1, SUBLANES, cb), lambda i, j, k: (i, 0, j))],
        out_shape=[jax.ShapeDtypeStruct((b * t, D_RNN), BF16),
                   jax.ShapeDtypeStruct((b, 1, D_RNN), F32),
                   jax.ShapeDtypeStruct((b, SUBLANES, D_RNN), F32)],
        scratch_shapes=[pltpu.VMEM((tt + SUBLANES, cb), F32),
                        pltpu.VMEM((tt // SUBLANES, SUBLANES, cb), F32),
                        pltpu.VMEM((tt // SUBLANES, SUBLANES, cb), F32),
                        pltpu.VMEM((tt // SUBLANES, SUBLANES, cb), F32),
                        pltpu.VMEM((1, cb), F32)],
        compiler_params=_params(3),
        name="rglru",
    )(zr, zr, conv_w, conv_b, w_rg_a, b_rg_a, w_rg_x, b_rg_x, lam, cinit, h0)


def _layer(xp3, xs3, mods_p, mods_s, p, src_p, src_s, cache_k, cache_v, conv_s, h0_s, tm):
    bp, tp, d = xp3.shape
    bs, ts, _ = xs3.shape
    mp, ms = bp * tp, bs * ts
    tm_big = 2 * tm
    assert tp % tm_big == 0
    sh1, sc1, g1, sh2, sc2, g2 = mods_p
    sh1s, sc1s, g1s, sh2s, sc2s, g2s = mods_s
    xp, xs = xp3.reshape(mp, d), xs3.reshape(ms, d)
    rows_s = lambda gate: jnp.broadcast_to(gate, (bs, ts, d)).reshape(ms, d)

    hp = _prenorm(xp3, sc1, sh1, tm).reshape(mp, d)
    hs = _prenorm(xs3, sc1s, sh1s, ts).reshape(ms, d)
    (qp,), (qs,) = _matmul([hp], [hs], [p['w_in']], [0], [0], D_ATTN, 1024, tm_big, _ep_plain, [BF16],
                           name="in_proj_q")
    (zp,), (zs,) = _matmul([hp], [hs], [p['w_in']], [0], [D_ATTN], 6 * 2048, 1024, tm_big, _ep_plain,
                           [F32], name="in_proj_rest")

    op = _attn_prompt(src_p, qp, zp, bp, tp)
    os_ = _attn_sample(src_s, qs, zs, cache_k, cache_v, bs, ts)
    rg = (p['conv_w'], p['conv_b'], p['w_rg_a'], p['b_rg_a'], p['w_rg_x'], p['b_rg_x'], p['rg_lambda'])
    yp, hlp, cop = _rglru(zp, bp, tp, 512, 512, *rg, jnp.zeros((bp, SUBLANES, D_RNN), F32),
                          jnp.zeros((bp, 1, D_RNN), F32))
    ys, hls, cos = _rglru(zs, bs, ts, ts, D_RNN, *rg, conv_s, h0_s)

    (mgp,), (mgs,) = _matmul([op, yp], [os_, ys], [p['w_proj_attn'], p['w_proj_rnn']], [0, 1], [0, 0],
                             D_MODEL, 512, tm_big, _ep_merge, [BF16], eins=[zp, zp],
                             e_col_off=[OFF_GA, OFF_GR], eins_x=[zs, zs], ex_col_off=[OFF_GA, OFF_GR],
                             name="branch_proj")
    (x1p,), (x1s,) = _matmul([mgp], [mgs], [p['w_out']], [0], [0], D_MODEL, 1024, tm_big, _ep_residual, [F32],
                             eins=[xp], e_col_off=[0], eins_x=[xs, rows_s(g1s)], ex_col_off=[0, 0],
                             vecs=[g1], rows_per_seq=tp, name="out_proj")

    h2p = _prenorm(x1p.reshape(bp, tp, d), sc2, sh2, tm).reshape(mp, d)
    h2s = _prenorm(x1s.reshape(bs, ts, d), sc2s, sh2s, ts).reshape(ms, d)
    d_ff = p['w_ffn_gate'].shape[1]
    (actp,), (acts,) = _matmul([h2p], [h2s], [p['w_ffn_gate'], p['w_ffn_up']], [0, 0], [0, 0], d_ff, 512,
                               tm_big, _ep_swiglu, [BF16], name="ffn_up")
    (xop,), (xos,) = _matmul([actp], [acts], [p['w_ffn_down']], [0], [0], D_MODEL, 512, tm, _ep_residual,
                             [F32], eins=[x1p], e_col_off=[0], eins_x=[x1s, rows_s(g2s)],
                             ex_col_off=[0, 0], vecs=[g2], rows_per_seq=tp, name="ffn_down")
    y_p = _final_norm(xop, p['final_norm_g'], tm).reshape(bp, tp, d)
    y_s = _final_norm(xos, p['final_norm_g'], ms).reshape(bs, ts, d)

    def states(z, b, t, h_last, cout):
        keep = min(BAND_LEFT, t)
        z3 = z.reshape(b, t, z.shape[1])
        new_k = z3[:, t - keep:, OFF_K:OFF_K + D_ATTN].reshape(b, keep, N_HEADS, HEAD_DIM)
        new_v = z3[:, t - keep:, OFF_V:OFF_V + D_ATTN].reshape(b, keep, N_HEADS, HEAD_DIM)
        return new_k, new_v, h_last.reshape(b, D_RNN), cout[:, SUBLANES - (CONV_WIDTH - 1):, :]

    return (y_p,) + states(zp, bp, tp, hlp, cop), (y_s,) + states(zs, bs, ts, hls, cos)


def kernel(x_prompt, x_sample, c_prompt, c_sample, cache_attn_k, cache_attn_v, state_rglru_h,
           state_rglru_conv, w_mod, b_mod, w_in, rel_bias, conv_w, conv_b, w_rg_a, b_rg_a, w_rg_x,
           b_rg_x, rg_lambda, w_proj_attn, w_proj_rnn, w_out, w_ffn_gate, w_ffn_up, w_ffn_down,
           final_norm_g):
    assert w_mod.shape[0] == 1, "single-layer step"
    bp, tp, d = x_prompt.shape
    bs, ts, _ = x_sample.shape
    p = {'w_in': w_in[0], 'conv_w': conv_w[0], 'conv_b': conv_b[0][None], 'w_rg_a': w_rg_a[0],
         'b_rg_a': b_rg_a[0][None], 'w_rg_x': w_rg_x[0], 'b_rg_x': b_rg_x[0][None],
         'rg_lambda': rg_lambda[0][None], 'w_proj_attn': w_proj_attn[0], 'w_proj_rnn': w_proj_rnn[0],
         'w_out': w_out[0], 'w_ffn_gate': w_ffn_gate[0], 'w_ffn_up': w_ffn_up[0],
         'w_ffn_down': w_ffn_down[0], 'final_norm_g': final_norm_g[None]}

    n_seq = bp + bs
    rows = -(-n_seq // 16) * 16
    c_all = jnp.concatenate([c_prompt, c_sample, jnp.zeros((rows - n_seq, d), F32)], axis=0)
    mod = _modulation(c_all, w_mod[0], b_mod[0][None])
    mods_p = [mod[:bp, i * d:(i + 1) * d][:, None, :] for i in range(6)]
    mods_s = [mod[bp:n_seq, i * d:(i + 1) * d][:, None, :] for i in range(6)]

    src = jnp.take(rel_bias[0], jnp.asarray(_bias_source_index()), axis=1)
    src_p = jnp.take(rel_bias[0], jnp.asarray(_prompt_bias_source_index()), axis=1)

    conv_state = jnp.pad(state_rglru_conv[0], ((0, 0), (SUBLANES - (CONV_WIDTH - 1), 0), (0, 0)))
    (yp, k1, v1, h1, b1), (ys, k2, v2, h2, b2) = _layer(
        x_prompt, x_sample, mods_p, mods_s, p, src_p[:, None, :], src, cache_attn_k[0], cache_attn_v[0],
        conv_state, state_rglru_h[0][:, None, :], tm=512)

    return (yp, ys, k1[None], v1[None], h1[None], b1[None], k2[None], v2[None], h2[None], b2[None])
```

```python
import functools

import numpy as np
import jax
import jax.numpy as jnp
from jax import lax
from jax.experimental import pallas as pl
from jax.experimental.pallas import tpu as pltpu

F32 = jnp.float32
BF16 = jnp.bfloat16

D_MODEL = 2048
CHUNK = 64
LEFT_CHUNKS = 8
BAND_LEFT = LEFT_CHUNKS * CHUNK
N_HEADS = 16
HEAD_DIM = 128
D_ATTN = N_HEADS * HEAD_DIM
MAX_REL = 256
N_REL = CHUNK + MAX_REL
D_RNN = 2048
RG_BLOCKS = 16
RG_BLOCK_DIM = D_RNN // RG_BLOCKS
CONV_WIDTH = 4
RG_C = 8.0
PAST_LEN = 1024
NORM_EPS = 1e-6
NEG_INF = -1e30
LOG2_E = 1.4426950408889634

LANES = 128
SUBLANES = 8
VMEM_LIMIT = 56 * 1024 * 1024

OFF_K, OFF_V, OFF_U, OFF_G, OFF_GA, OFF_GR = (i * 2048 for i in range(6))

Q_PAIR = 2 * CHUNK
K_WIN = BAND_LEFT + Q_PAIR
BIAS_W = K_WIN + LANES
Q_BLK = 4 * CHUNK
P_WIN = BAND_LEFT + Q_BLK
P_SRC_W = P_WIN + Q_BLK
T_XPOSE = 512


def _params(n_axes, flags=None):
    return pltpu.CompilerParams(dimension_semantics=("arbitrary",) * n_axes,
                                vmem_limit_bytes=VMEM_LIMIT, flags=flags)


def _mod_kernel(c_ref, w_ref, b_ref, o_ref):
    c = c_ref[...]
    s = (c * jax.nn.sigmoid(c)).astype(BF16)
    o_ref[...] = jnp.dot(s, w_ref[...].astype(BF16), preferred_element_type=F32) + b_ref[...]


def _modulation(c_all, w_mod, b_mod, tn=1024):
    rows, d = c_all.shape
    n = w_mod.shape[1]
    return pl.pallas_call(
        _mod_kernel,
        grid=(n // tn,),
        in_specs=[pl.BlockSpec((rows, d), lambda j: (0, 0)),
                  pl.BlockSpec((d, tn), lambda j: (0, j)),
                  pl.BlockSpec((1, tn), lambda j: (0, j))],
        out_specs=pl.BlockSpec((rows, tn), lambda j: (0, j)),
        out_shape=jax.ShapeDtypeStruct((rows, n), F32),
        compiler_params=_params(1),
        name="adaln_mod",
    )(c_all, w_mod, b_mod)


def _rms(xf):
    return xf * lax.rsqrt(jnp.mean(xf * xf, axis=-1, keepdims=True) + NORM_EPS)


def _prenorm_kernel(x_ref, sc_ref, sh_ref, o_ref):
    o_ref[0] = (_rms(x_ref[0]) * (1.0 + sc_ref[0]) + sh_ref[0]).astype(o_ref.dtype)


def _prenorm(x3, sc, sh, tm):
    b, t, d = x3.shape
    return pl.pallas_call(
        _prenorm_kernel,
        grid=(b, t // tm),
        in_specs=[pl.BlockSpec((1, tm, d), lambda i, j: (i, j, 0)),
                  pl.BlockSpec((1, 1, d), lambda i, j: (i, 0, 0)),
                  pl.BlockSpec((1, 1, d), lambda i, j: (i, 0, 0))],
        out_specs=pl.BlockSpec((1, tm, d), lambda i, j: (i, j, 0)),
        out_shape=jax.ShapeDtypeStruct((b, t, d), BF16),
        compiler_params=_params(2),
        name="prenorm",
    )(x3, sc, sh)


def _final_norm_kernel(x_ref, g_ref, o_ref):
    o_ref[...] = _rms(x_ref[...]) * g_ref[...]


def _final_norm(x2, gain, tm):
    m, d = x2.shape
    return pl.pallas_call(
        _final_norm_kernel,
        grid=(m // tm,),
        in_specs=[pl.BlockSpec((tm, d), lambda i: (i, 0)),
                  pl.BlockSpec((1, d), lambda i: (0, 0))],
        out_specs=pl.BlockSpec((tm, d), lambda i: (i, 0)),
        out_shape=jax.ShapeDtypeStruct((m, d), F32),
        compiler_params=_params(1),
        name="final_norm",
    )(x2, gain)


def _mm_kernel(*refs, n_a, a_of_w, n_e, n_ex, n_v, n_out, epilogue):
    it = iter(refs)
    take = lambda k: [next(it) for _ in range(k)]
    n_w = len(a_of_w)
    a_refs, ax_refs, w_refs = take(n_a), take(n_a), take(n_w)
    e_refs, ex_refs, v_refs = take(n_e), take(n_ex), take(n_v)
    o_refs, ox_refs, wbf_refs = take(n_out), take(n_out), take(n_w)

    def run(act_refs, ein_refs, vec_vals, out_refs):
        accs = [jnp.dot(act_refs[a_of_w[i]][...], wbf_refs[i][...], preferred_element_type=F32)
                for i in range(n_w)]
        outs = epilogue(accs, [e[...] for e in ein_refs], vec_vals)
        for o_ref, val in zip(out_refs, outs):
            o_ref[...] = val.astype(o_ref.dtype)

    @pl.when(pl.program_id(1) == 0)
    def _():
        for w_ref, wbf_ref in zip(w_refs, wbf_refs):
            wbf_ref[...] = w_ref[...].astype(BF16)
        run(ax_refs, ex_refs, [], ox_refs)

    run(a_refs, e_refs, [v[0] for v in v_refs], o_refs)


def _matmul(acts, acts_x, weights, a_of_w, w_col_off, n, tn, tm, epilogue, out_dtypes,
            eins=(), e_col_off=(), eins_x=(), ex_col_off=(), vecs=(), rows_per_seq=None, name="mm"):
    m = acts[0].shape[0]
    mx = acts_x[0].shape[0]
    in_specs = []
    for a in acts:
        in_specs.append(pl.BlockSpec((tm, a.shape[1]), lambda j, i: (i, 0)))
    for a in acts_x:
        in_specs.append(pl.BlockSpec((mx, a.shape[1]), lambda j, i: (0, 0)))
    for w, off in zip(weights, w_col_off):
        in_specs.append(pl.BlockSpec((w.shape[0], tn), lambda j, i, o=off // tn: (0, o + j)))
    for e, off in zip(eins, e_col_off):
        in_specs.append(pl.BlockSpec((tm, tn), lambda j, i, o=off // tn: (i, o + j)))
    for e, off in zip(eins_x, ex_col_off):
        in_specs.append(pl.BlockSpec((mx, tn), lambda j, i, o=off // tn: (0, o + j)))
    for v in vecs:
        in_specs.append(pl.BlockSpec((1, 1, tn), lambda j, i: ((i * tm) // rows_per_seq, 0, j)))
    kern = functools.partial(_mm_kernel, n_a=len(acts), a_of_w=tuple(a_of_w), n_e=len(eins),
                             n_ex=len(eins_x), n_v=len(vecs), n_out=len(out_dtypes), epilogue=epilogue)
    outs = pl.pallas_call(
        kern,
        grid=(n // tn, m // tm),
        in_specs=in_specs,
        out_specs=([pl.BlockSpec((tm, tn), lambda j, i: (i, j)) for _ in out_dtypes]
                   + [pl.BlockSpec((mx, tn), lambda j, i: (0, j)) for _ in out_dtypes]),
        out_shape=([jax.ShapeDtypeStruct((m, n), dt) for dt in out_dtypes]
                   + [jax.ShapeDtypeStruct((mx, n), dt) for dt in out_dtypes]),
        scratch_shapes=[pltpu.VMEM((w.shape[0], tn), BF16) for w in weights],
        compiler_params=_params(2),
        name=name,
    )(*acts, *acts_x, *weights, *eins, *eins_x, *vecs)
    return outs[:len(out_dtypes)], outs[len(out_dtypes):]


def _ep_plain(accs, eins, vecs):
    return [accs[0]]


def _ep_merge(accs, eins, vecs):
    ta = jnp.tanh(0.5 * eins[0])
    tr = jnp.tanh(0.5 * eins[1])
    return [0.5 * ((1.0 + ta) * accs[0] + (1.0 + tr) * accs[1])]


def _ep_residual(accs, eins, vecs):
    gate = vecs[0] if vecs else eins[1]
    return [eins[0] + gate * accs[0]]


def _ep_swiglu(accs, eins, vecs):
    hg = 0.5 * accs[0]
    return [(hg + hg * jnp.tanh(hg)) * accs[1]]


def _bias_source_index():
    m = np.arange(BIAS_W)
    rel = np.where(m < K_WIN, BAND_LEFT - m, MAX_REL)
    return np.clip(rel, -(CHUNK - 1), MAX_REL) + (CHUNK - 1)


def _toeplitz_bias(src_row, n_rows):
    t = jnp.broadcast_to(src_row, (n_rows, BIAS_W))
    ri = lax.broadcasted_iota(jnp.int32, (n_rows, BIAS_W), 0)
    for bit in range(max(1, (n_rows - 1).bit_length())):
        t = jnp.where(((ri >> bit) & 1) == 1, pltpu.roll(t, 1 << bit, axis=1), t)
    return t[:, :K_WIN]


def _softmax_pv(s, vw):
    mx = jnp.max(s, axis=-1, keepdims=True)
    e = jnp.exp(s - mx)
    l = jnp.sum(e, axis=-1, keepdims=True)
    pv = jnp.dot(e.astype(BF16), vw, preferred_element_type=F32)
    return pv / l


def _prompt_bias_source_index():
    m = np.arange(P_SRC_W)
    rel = np.where(m < Q_BLK, MAX_REL, m - P_SRC_W + BAND_LEFT)
    return np.clip(rel, -(CHUNK - 1), MAX_REL) + (CHUNK - 1)


def _attn_prompt_kernel(src_ref, mask_ref, q_ref, k_ref, v_ref, o_ref, kpad, vtpad, bias_scr, st_scr,
                        *, t):
    @pl.when(pl.program_id(1) == 0)
    def _():
        r8 = jnp.broadcast_to(src_ref[0], (SUBLANES, P_SRC_W))
        ri = lax.broadcasted_iota(jnp.int32, (SUBLANES, P_SRC_W), 0)
        for bit in range(3):
            r8 = jnp.where(((ri >> bit) & 1) == 1, pltpu.roll(r8, 1 << bit, axis=1), r8)
        ext = jnp.concatenate([r8, r8[:, :Q_BLK]], axis=1)
        ext_w = P_SRC_W + Q_BLK
        for t_off in range(0, LANES, SUBLANES):
            rot = ext if t_off == 0 else pltpu.roll(ext, ext_w - t_off, axis=1)
            for a in range(P_WIN // SUBLANES):
                start = (P_SRC_W - SUBLANES * a) % P_SRC_W
                if start % LANES != t_off:
                    continue
                rows = slice(SUBLANES * a, SUBLANES * (a + 1))
                col = start - t_off
                bias_scr[rows, :] = (rot[:, col:col + Q_BLK] + mask_ref[rows, :]) * LOG2_E

    kpad[0:BAND_LEFT, :] = jnp.zeros((BAND_LEFT, HEAD_DIM), BF16)
    vtpad[:, 0:BAND_LEFT] = jnp.zeros((HEAD_DIM, BAND_LEFT), BF16)
    kpad[BAND_LEFT:, :] = k_ref[...].astype(BF16)
    for c in range(t // T_XPOSE):
        rows = slice(c * T_XPOSE, (c + 1) * T_XPOSE)
        vtpad[:, BAND_LEFT + c * T_XPOSE:BAND_LEFT + (c + 1) * T_XPOSE] = v_ref[rows, :].T.astype(BF16)
    row = lax.broadcasted_iota(jnp.int32, (P_WIN, Q_BLK), 0)
    scale = HEAD_DIM ** -0.5
    n_blk = t // Q_BLK
    half = P_WIN // 2

    def scores(blk, slot):
        r0 = pl.multiple_of(blk * Q_BLK, Q_BLK)
        q = q_ref[pl.ds(r0, Q_BLK), :]
        for hh in range(2):
            kw = kpad[pl.ds(pl.multiple_of(r0 + hh * half, LANES), half), :]
            st_scr[slot, hh * half:(hh + 1) * half, :] = lax.dot_general(
                kw, q, (((1,), (1,)), ((), ())), preferred_element_type=F32)

    def finish(blk, slot, clip_start):
        r0 = pl.multiple_of(blk * Q_BLK, Q_BLK)
        st = st_scr[slot] * (scale * LOG2_E) + bias_scr[...]
        if clip_start:
            st = jnp.where(row >= BAND_LEFT - blk * Q_BLK, st, NEG_INF)
        mx = jnp.max(st, axis=0, keepdims=True)
        e = jnp.exp2(st - mx)
        l = jnp.sum(e, axis=0, keepdims=True)
        vtw = vtpad[:, pl.ds(r0, P_WIN)]
        ot = jnp.dot(vtw, e.astype(BF16), preferred_element_type=F32) / l
        o_ref[pl.ds(r0, Q_BLK), :] = ot.T.astype(o_ref.dtype)

    def pair(j, carry, *, clip_start):
        scores(2 * j + 1, 1)
        finish(2 * j, 0, clip_start)
        scores(jnp.minimum(2 * j + 2, n_blk - 1), 0)
        finish(2 * j + 1, 1, clip_start)
        return carry

    assert n_blk % 2 == 0 and BAND_LEFT == 2 * Q_BLK
    scores(0, 0)
    pair(0, 0, clip_start=True)
    lax.fori_loop(1, n_blk // 2, functools.partial(pair, clip_start=False), 0, unroll=True)


def _prompt_band_mask():
    j = np.arange(P_WIN)[:, None] // CHUNK
    i = np.arange(Q_BLK)[None, :] // CHUNK
    d = i + LEFT_CHUNKS - j
    return np.where((d >= 0) & (d <= LEFT_CHUNKS), 0.0, NEG_INF).astype(np.float32)


def _attn_prompt(src, q, zr, b, t):
    assert t % Q_BLK == 0 and t % T_XPOSE == 0
    mask = jnp.asarray(_prompt_band_mask())
    kern = functools.partial(_attn_prompt_kernel, t=t)
    return pl.pallas_call(
        kern,
        grid=(N_HEADS, b),
        in_specs=[pl.BlockSpec((1, 1, P_SRC_W), lambda h, i: (h, 0, 0)),
                  pl.BlockSpec((P_WIN, Q_BLK), lambda h, i: (0, 0)),
                  pl.BlockSpec((t, HEAD_DIM), lambda h, i: (i, h)),
                  pl.BlockSpec((t, HEAD_DIM), lambda h, i: (i, OFF_K // HEAD_DIM + h)),
                  pl.BlockSpec((t, HEAD_DIM), lambda h, i: (i, OFF_V // HEAD_DIM + h))],
        out_specs=pl.BlockSpec((t, HEAD_DIM), lambda h, i: (i, h)),
        out_shape=jax.ShapeDtypeStruct((b * t, D_ATTN), BF16),
        scratch_shapes=[pltpu.VMEM((BAND_LEFT + t, HEAD_DIM), BF16),
                        pltpu.VMEM((HEAD_DIM, BAND_LEFT + t), BF16),
                        pltpu.VMEM((P_WIN, Q_BLK), F32),
                        pltpu.VMEM((2, P_WIN, Q_BLK), F32)],
        compiler_params=_params(2),
        name="attn_prompt",
    )(src, mask, q, zr, zr)


def _attn_sample_kernel(src_ref, mask_ref, q_ref, kn_ref, vn_ref, ck_hbm, cv_hbm, o_ref, kbuf, vbuf, sems,
                        bias_scr, *, t, r, n_seq):
    seq = pl.program_id(0)
    slot = seq % 2

    def seq_copies(s_idx, s_slot):
        cps = []
        for h in range(N_HEADS):
            cps.append(pltpu.make_async_copy(ck_hbm.at[s_idx, :, h, :], kbuf.at[s_slot, h], sems.at[0, s_slot]))
            cps.append(pltpu.make_async_copy(cv_hbm.at[s_idx, :, h, :], vbuf.at[s_slot, h], sems.at[1, s_slot]))
        return cps

    @pl.when(seq == 0)
    def _():
        for cp in seq_copies(0, 0):
            cp.start()
        for h in range(N_HEADS):
            bias_scr[h] = _toeplitz_bias(src_ref[h:h + 1, :], t) + mask_ref[...]

    @pl.when(seq + 1 < n_seq)
    def _():
        for cp in seq_copies(seq + 1, 1 - slot):
            cp.start()

    for cp in seq_copies(seq, slot):
        cp.wait()

    scale = HEAD_DIM ** -0.5
    nt = (((1,), (1,)), ((), ()))
    heads = range(N_HEADS)
    cols = [slice(h * HEAD_DIM, (h + 1) * HEAD_DIM) for h in heads]
    s_c = [lax.dot_general(q_ref[:, cols[h]], kbuf[slot, h].astype(BF16), nt, preferred_element_type=F32)
           for h in heads]
    s_n = [lax.dot_general(q_ref[:, cols[h]], kn_ref[:, cols[h]].astype(BF16), nt,
                           preferred_element_type=F32) for h in heads]
    s_c = [s_c[h] * scale + bias_scr[h, :, 0:r] for h in heads]
    s_n = [s_n[h] * scale + bias_scr[h, :, r:r + t] for h in heads]
    mx = [jnp.maximum(jnp.max(s_c[h], axis=-1, keepdims=True), jnp.max(s_n[h], axis=-1, keepdims=True))
          for h in heads]
    e_c = [jnp.exp(s_c[h] - mx[h]) for h in heads]
    e_n = [jnp.exp(s_n[h] - mx[h]) for h in heads]
    l = [jnp.sum(e_c[h], axis=-1, keepdims=True) + jnp.sum(e_n[h], axis=-1, keepdims=True) for h in heads]
    pv = [jnp.dot(e_c[h].astype(BF16), vbuf[slot, h].astype(BF16), preferred_element_type=F32)
          + jnp.dot(e_n[h].astype(BF16), vn_ref[:, cols[h]].astype(BF16), preferred_element_type=F32)
          for h in heads]
    for h in heads:
        o_ref[:, cols[h]] = (pv[h] / l[h]).astype(o_ref.dtype)


def _sample_mask(t, r):
    q_pos = PAST_LEN + np.arange(t)[:, None]
    j = np.arange(K_WIN)[None, :]
    k_pos = PAST_LEN - r + j
    d = q_pos // CHUNK - k_pos // CHUNK
    valid = (j < r + t) & (k_pos >= 0) & (d >= 0) & (d <= LEFT_CHUNKS)
    return np.where(valid, 0.0, NEG_INF).astype(np.float32)


def _attn_sample(src, q, zr, cache_k, cache_v, b, t):
    r = cache_k.shape[1]
    assert r == BAND_LEFT and r + t <= K_WIN and t % 16 == 0
    mask = jnp.asarray(_sample_mask(t, r))
    kern = functools.partial(_attn_sample_kernel, t=t, r=r, n_seq=b)
    return pl.pallas_call(
        kern,
        grid=(b,),
        in_specs=[pl.BlockSpec((N_HEADS, BIAS_W), lambda i: (0, 0)),
                  pl.BlockSpec((t, K_WIN), lambda i: (0, 0)),
                  pl.BlockSpec((t, D_ATTN), lambda i: (i, 0)),
                  pl.BlockSpec((t, D_ATTN), lambda i: (i, OFF_K // D_ATTN)),
                  pl.BlockSpec((t, D_ATTN), lambda i: (i, OFF_V // D_ATTN)),
                  pl.BlockSpec(memory_space=pl.ANY),
                  pl.BlockSpec(memory_space=pl.ANY)],
        out_specs=pl.BlockSpec((t, D_ATTN), lambda i: (i, 0)),
        out_shape=jax.ShapeDtypeStruct((b * t, D_ATTN), BF16),
        scratch_shapes=[pltpu.VMEM((2, N_HEADS, r, HEAD_DIM), F32),
                        pltpu.VMEM((2, N_HEADS, r, HEAD_DIM), F32),
                        pltpu.SemaphoreType.DMA((2, 2)),
                        pltpu.VMEM((N_HEADS, t, K_WIN), F32)],
        compiler_params=_params(1),
        name="attn_sample",
    )(src, mask, q, zr, zr, cache_k, cache_v)


def _softplus(x):
    return jnp.maximum(x, 0.0) + jnp.log1p(jnp.exp(-jnp.abs(x)))


def _rglru_kernel(*refs, tt, cb, side):
    it = iter(refs)
    take = lambda k: [next(it) for _ in range(k)]
    (u_ref, g_ref, cw_ref, cb_ref, wa_ref, ba_ref, wx_ref, bx_ref, lam_ref, cinit_ref, h0_ref) = take(11)
    sa_ref, sax_ref, sw_ref = take(3) if side else (None, None, None)
    y_ref, hl_ref, cout_ref = take(3)
    so_ref, sox_ref = take(2) if side else (None, None)
    ext, a_scr, b_scr, h_scr, hcarry = take(5)
    swbf, sacc = take(2) if side else (None, None)
    ti = pl.program_id(2)
    pad = SUBLANES

    @pl.when(ti == 0)
    def _():
        ext[0:pad, :] = cinit_ref[0]
        hcarry[...] = h0_ref[0]
        if side:
            swbf[...] = sw_ref[...].astype(BF16)
            sox_ref[...] = jnp.dot(sax_ref[...], swbf[...], preferred_element_type=F32).astype(sox_ref.dtype)

    @pl.when(ti > 0)
    def _():
        ext[0:pad, :] = ext[tt:tt + pad, :]

    ext[pad:pad + tt, :] = u_ref[...]
    groups = tt // SUBLANES
    n_blk = cb // RG_BLOCK_DIM
    bw = RG_BLOCK_DIM
    si = lax.broadcasted_iota(jnp.int32, (groups, SUBLANES, bw), 1)
    for n in range(n_blk):
        cs = slice(n * bw, (n + 1) * bw)
        cw = 0.5 * cw_ref[:, cs]
        xh = 0.5 * cb_ref[:, cs] + ext[pad:pad + tt, cs] * cw[CONV_WIDTH - 1:CONV_WIDTH, :]
        for j in range(CONV_WIDTH - 1):
            back = CONV_WIDTH - 1 - j
            xh = xh + ext[pad - back:pad - back + tt, cs] * cw[j:j + 1, :]

        xhb = xh.astype(BF16)
        zr_half = jnp.dot(xhb, wa_ref[n].astype(BF16), preferred_element_type=F32)
        zi_half = jnp.dot(xhb, wx_ref[n].astype(BF16), preferred_element_type=F32)
        tr = jnp.tanh(zr_half + 0.5 * ba_ref[:, cs])
        ti = jnp.tanh(zi_half + 0.5 * bx_ref[:, cs])
        log_a = (1.0 + tr) * ((-0.5 * RG_C) * _softplus(-lam_ref[:, cs]))
        a = jnp.exp(log_a)
        v = jnp.tanh(log_a) * (-1.0 - a * a)
        root = jnp.where(v > 0.0, v * lax.rsqrt(v), 0.0)
        b = root * ((1.0 + ti) * xh)

        if side:
            kw = sa_ref.shape[1] // n_blk
            part = jnp.dot(sa_ref[:, n * kw:(n + 1) * kw], swbf[n * kw:(n + 1) * kw, :],
                           preferred_element_type=F32)
            if n == 0:
                sacc[...] = part
            elif n < n_blk - 1:
                sacc[...] += part
            else:
                so_ref[...] = (sacc[...] + part).astype(so_ref.dtype)

        a3 = a.reshape(groups, SUBLANES, bw)
        b3 = b.reshape(groups, SUBLANES, bw)
        for k in (1, 2, 4):
            keep = si >= k
            a_prev = pltpu.roll(a3, k, axis=1)
            b_prev = pltpu.roll(b3, k, axis=1)
            b3 = jnp.where(keep, a3 * b_prev + b3, b3)
            a3 = jnp.where(keep, a3 * a_prev, a3)
        a_scr[:, :, cs] = a3
        b_scr[:, :, cs] = b3

    def step(gi, h_prev):
        hg = a_scr[gi] * h_prev + b_scr[gi]
        h_scr[gi] = hg
        return hg[SUBLANES - 1:SUBLANES, :]

    h_last = lax.fori_loop(0, groups, step, hcarry[...], unroll=min(groups, 8))
    hcarry[...] = h_last
    h = h_scr[...].reshape(tt, cb)
    y_ref[...] = (h * jax.nn.gelu(g_ref[...])).astype(y_ref.dtype)
    hl_ref[0] = h_last
    cout_ref[0] = ext[tt:tt + pad, :]


def _rglru(zr, b, t, tt, cb, conv_w, conv_b, w_rg_a, b_rg_a, w_rg_x, b_rg_x, lam, cinit, h0, side=None):
    assert t % tt == 0 and tt % SUBLANES == 0 and D_RNN % cb == 0 and cb % RG_BLOCK_DIM == 0
    nt = t // tt
    nc = D_RNN // cb
    nb = cb // RG_BLOCK_DIM
    vec = lambda: pl.BlockSpec((1, cb), lambda i, j, k: (0, j))
    kern = functools.partial(_rglru_kernel, tt=tt, cb=cb, side=side is not None)
    side_in, side_specs, side_out_specs, side_out_shape, side_scratch = [], [], [], [], []
    if side is not None:
        act, act_x, w, n = side
        m, kdim = act.shape
        mx = act_x.shape[0]
        sm, sn = m // nt, n // (b * nc)
        assert m % nt == 0 and n % (b * nc) == 0 and sn % LANES == 0 and sm % 16 == 0
        side_in = [act, act_x, w]
        side_specs = [pl.BlockSpec((sm, kdim), lambda i, j, k: (k, 0)),
                      pl.BlockSpec((mx, kdim), lambda i, j, k: (0, 0)),
                      pl.BlockSpec((kdim, sn), lambda i, j, k: (0, i * nc + j))]
        side_out_specs = [pl.BlockSpec((sm, sn), lambda i, j, k: (k, i * nc + j)),
                          pl.BlockSpec((mx, sn), lambda i, j, k: (0, i * nc + j))]
        side_out_shape = [jax.ShapeDtypeStruct((m, n), BF16), jax.ShapeDtypeStruct((mx, n), BF16)]
        assert kdim % (nb * LANES) == 0
        side_scratch = [pltpu.VMEM((kdim, sn), BF16), pltpu.VMEM((sm, sn), F32)]
    return pl.pallas_call(
        kern,
        grid=(b, nc, nt),
        in_specs=[pl.BlockSpec((tt, cb), lambda i, j, k: (i * nt + k, OFF_U // cb + j)),
                  pl.BlockSpec((tt, cb), lambda i, j, k: (i * nt + k, OFF_G // cb + j)),
                  pl.BlockSpec((CONV_WIDTH, cb), lambda i, j, k: (0, j)),
                  vec(),
                  pl.BlockSpec((nb, RG_BLOCK_DIM, RG_BLOCK_DIM), lambda i, j, k: (j, 0, 0)),
                  vec(),
                  pl.BlockSpec((nb, RG_BLOCK_DIM, RG_BLOCK_DIM), lambda i, j, k: (j, 0, 0)),
                  vec(),
                  vec(),
                  pl.BlockSpec((1, SUBLANES, cb), lambda i, j, k: (i, 0, j)),
                  pl.BlockSpec((1, 1, cb), lambda i, j, k: (i, 0, j))] + side_specs,
        out_specs=[pl.BlockSpec((tt, cb), lambda i, j, k: (i * nt + k, j)),
                   pl.BlockSpec((1, 1, cb), lambda i, j, k: (i, 0, j)),
                   pl.BlockSpec((1, SUBLANES, cb), lambda i, j, k: (i, 0, j))] + side_out_specs,
        out_shape=[jax.ShapeDtypeStruct((b * t, D_RNN), BF16),
                   jax.ShapeDtypeStruct((b, 1, D_RNN), F32),
                   jax.ShapeDtypeStruct((b, SUBLANES, D_RNN), F32)] + side_out_shape,
        scratch_shapes=[pltpu.VMEM((tt + SUBLANES, cb), F32),
                        pltpu.VMEM((tt // SUBLANES, SUBLANES, cb), F32),
                        pltpu.VMEM((tt // SUBLANES, SUBLANES, cb), F32),
                        pltpu.VMEM((tt // SUBLANES, SUBLANES, cb), F32),
                        pltpu.VMEM((1, cb), F32)] + side_scratch,
        compiler_params=_params(3),
        name="rglru",
    )(zr, zr, conv_w, conv_b, w_rg_a, b_rg_a, w_rg_x, b_rg_x, lam, cinit, h0, *side_in)


def _layer(xp3, xs3, mods_p, mods_s, p, src_p, src_s, cache_k, cache_v, conv_s, h0_s, tm):
    bp, tp, d = xp3.shape
    bs, ts, _ = xs3.shape
    mp, ms = bp * tp, bs * ts
    tm_big = 2 * tm
    assert tp % tm_big == 0
    sh1, sc1, g1, sh2, sc2, g2 = mods_p
    sh1s, sc1s, g1s, sh2s, sc2s, g2s = mods_s
    xp, xs = xp3.reshape(mp, d), xs3.reshape(ms, d)
    rows_s = lambda gate: jnp.broadcast_to(gate, (bs, ts, d)).reshape(ms, d)

    hp = _prenorm(xp3, sc1, sh1, tm).reshape(mp, d)
    hs = _prenorm(xs3, sc1s, sh1s, ts).reshape(ms, d)
    (zp,), (zs,) = _matmul([hp], [hs], [p['w_in']], [0], [D_ATTN], 6 * 2048, 1024, tm_big, _ep_plain,
                           [F32], name="in_proj_rest")

    rg = (p['conv_w'], p['conv_b'], p['w_rg_a'], p['b_rg_a'], p['w_rg_x'], p['b_rg_x'], p['rg_lambda'])
    yp, hlp, cop, qp, qs = _rglru(zp, bp, tp, 512, 512, *rg, jnp.zeros((bp, SUBLANES, D_RNN), F32),
                                  jnp.zeros((bp, 1, D_RNN), F32), side=(hp, hs, p['w_in'], D_ATTN))
    ys, hls, cos = _rglru(zs, bs, ts, ts, D_RNN, *rg, conv_s, h0_s)
    op = _attn_prompt(src_p, qp, zp, bp, tp)
    os_ = _attn_sample(src_s, qs, zs, cache_k, cache_v, bs, ts)

    (mgp,), (mgs,) = _matmul([op, yp], [os_, ys], [p['w_proj_attn'], p['w_proj_rnn']], [0, 1], [0, 0],
                             D_MODEL, 512, tm_big, _ep_merge, [BF16], eins=[zp, zp],
                             e_col_off=[OFF_GA, OFF_GR], eins_x=[zs, zs], ex_col_off=[OFF_GA, OFF_GR],
                             name="branch_proj")
    (x1p,), (x1s,) = _matmul([mgp], [mgs], [p['w_out']], [0], [0], D_MODEL, 1024, tm_big, _ep_residual, [F32],
                             eins=[xp], e_col_off=[0], eins_x=[xs, rows_s(g1s)], ex_col_off=[0, 0],
                             vecs=[g1], rows_per_seq=tp, name="out_proj")

    h2p = _prenorm(x1p.reshape(bp, tp, d), sc2, sh2, tm).reshape(mp, d)
    h2s = _prenorm(x1s.reshape(bs, ts, d), sc2s, sh2s, ts).reshape(ms, d)
    d_ff = p['w_ffn_gate'].shape[1]
    (actp,), (acts,) = _matmul([h2p], [h2s], [p['w_ffn_gate'], p['w_ffn_up']], [0, 0], [0, 0], d_ff, 512,
                               tm_big, _ep_swiglu, [BF16], name="ffn_up")
    (xop,), (xos,) = _matmul([actp], [acts], [p['w_ffn_down']], [0], [0], D_MODEL, 512, tm, _ep_residual,
                             [F32], eins=[x1p], e_col_off=[0], eins_x=[x1s, rows_s(g2s)],
                             ex_col_off=[0, 0], vecs=[g2], rows_per_seq=tp, name="ffn_down")
    y_p = _final_norm(xop, p['final_norm_g'], tm).reshape(bp, tp, d)
    y_s = _final_norm(xos, p['final_norm_g'], ms).reshape(bs, ts, d)

    def states(z, b, t, h_last, cout):
        keep = min(BAND_LEFT, t)
        z3 = z.reshape(b, t, z.shape[1])
        new_k = z3[:, t - keep:, OFF_K:OFF_K + D_ATTN].reshape(b, keep, N_HEADS, HEAD_DIM)
        new_v = z3[:, t - keep:, OFF_V:OFF_V + D_ATTN].reshape(b, keep, N_HEADS, HEAD_DIM)
        return new_k, new_v, h_last.reshape(b, D_RNN), cout[:, SUBLANES - (CONV_WIDTH - 1):, :]

    return (y_p,) + states(zp, bp, tp, hlp, cop), (y_s,) + states(zs, bs, ts, hls, cos)


def kernel(x_prompt, x_sample, c_prompt, c_sample, cache_attn_k, cache_attn_v, state_rglru_h,
           state_rglru_conv, w_mod, b_mod, w_in, rel_bias, conv_w, conv_b, w_rg_a, b_rg_a, w_rg_x,
           b_rg_x, rg_lambda, w_proj_attn, w_proj_rnn, w_out, w_ffn_gate, w_ffn_up, w_ffn_down,
           final_norm_g):
    assert w_mod.shape[0] == 1, "single-layer step"
    bp, tp, d = x_prompt.shape
    bs, ts, _ = x_sample.shape
    p = {'w_in': w_in[0], 'conv_w': conv_w[0], 'conv_b': conv_b[0][None], 'w_rg_a': w_rg_a[0],
         'b_rg_a': b_rg_a[0][None], 'w_rg_x': w_rg_x[0], 'b_rg_x': b_rg_x[0][None],
         'rg_lambda': rg_lambda[0][None], 'w_proj_attn': w_proj_attn[0], 'w_proj_rnn': w_proj_rnn[0],
         'w_out': w_out[0], 'w_ffn_gate': w_ffn_gate[0], 'w_ffn_up': w_ffn_up[0],
         'w_ffn_down': w_ffn_down[0], 'final_norm_g': final_norm_g[None]}

    n_seq = bp + bs
    rows = -(-n_seq // 16) * 16
    c_all = jnp.concatenate([c_prompt, c_sample, jnp.zeros((rows - n_seq, d), F32)], axis=0)
    mod = _modulation(c_all, w_mod[0], b_mod[0][None])
    mods_p = [mod[:bp, i * d:(i + 1) * d][:, None, :] for i in range(6)]
    mods_s = [mod[bp:n_seq, i * d:(i + 1) * d][:, None, :] for i in range(6)]

    src = jnp.take(rel_bias[0], jnp.asarray(_bias_source_index()), axis=1)
    src_p = jnp.take(rel_bias[0], jnp.asarray(_prompt_bias_source_index()), axis=1)

    conv_state = jnp.pad(state_rglru_conv[0], ((0, 0), (SUBLANES - (CONV_WIDTH - 1), 0), (0, 0)))
    (yp, k1, v1, h1, b1), (ys, k2, v2, h2, b2) = _layer(
        x_prompt, x_sample, mods_p, mods_s, p, src_p[:, None, :], src, cache_attn_k[0], cache_attn_v[0],
        conv_state, state_rglru_h[0][:, None, :], tm=512)

    return (yp, ys, k1[None], v1[None], h1[None], b1[None], k2[None], v2[None], h2[None], b2[None])
```
